```python
import jax, jax.numpy as jnp
from jax import lax
import numpy as np

D_MODEL = 4096
BATCH = 4
SEQ = 2048
DEPTH = 2
DEC_BATCH = 128
DEC_SEQ = 4
PAST_LEN = 16384
PAGE_SIZE = 128

D_MIX = 2 * D_MODEL
W_A = D_MIX // 2
W_B = D_MIX - W_A
GM_CHUNK = 128
A_HEAD = 128
H_A = W_A // A_HEAD
SSM_HEAD_DIM = 64
H_B = W_B // SSM_HEAD_DIM
D_STATE = 128
N_GROUPS = 8
HEADS_PER_GROUP = H_B // N_GROUPS
CONV_W = 4
CONV_DIM = W_B + 2 * N_GROUPS * D_STATE
SSD_CHUNK = 128
D_FF = 256 * ((8 * D_MODEL // 3 + 255) // 256)
IN_COLS = 2 * W_A + W_B + CONV_DIM + H_B
N_SUB = 3
FFN_RES = 0.5
EPS = 1e-6

kernel_name = "hymba_gmlp_ssd_macaron_adaln_decoder_step"


def rmsnorm(x, g):
    xf = x.astype(jnp.float32)
    y = xf * lax.rsqrt(jnp.mean(xf * xf, axis=-1, keepdims=True) + EPS) * g.astype(jnp.float32)
    return y.astype(x.dtype)


def swiglu(h, w_in, w_out):
    gate, up = jnp.split(h @ w_in, 2, axis=-1)
    return (jax.nn.silu(gate) * up) @ w_out


def gmlp_spatial_gate(u, v, ln_g, ln_b, ws, bs):
    b, L, _ = v.shape
    u = jax.nn.gelu(u)
    vf = jax.nn.gelu(v).astype(jnp.float32)
    mu = jnp.mean(vf, axis=-1, keepdims=True)
    var = jnp.mean(jnp.square(vf - mu), axis=-1, keepdims=True)
    vn = ((vf - mu) * lax.rsqrt(var + EPS) * ln_g + ln_b).astype(v.dtype)
    nc = -(-L // GM_CHUNK)
    pad = nc * GM_CHUNK - L
    vc = jnp.pad(vn, ((0, 0), (0, pad), (0, 0))).reshape(b, nc, GM_CHUNK, H_A, A_HEAD)
    f = jnp.einsum('hts,bcshe->bcthe', jnp.tril(ws), vc) + jnp.swapaxes(bs, 0, 1)[None, None, :, :, None]
    f = f.reshape(b, nc * GM_CHUNK, W_A)[:, :L]
    return u * f, vn


def ssd_scan(x, dt, A, Bm, Cm, h0, chunk):
    b, L, H, P = x.shape
    nc = L // chunk
    G, hg = N_GROUPS, HEADS_PER_GROUP
    x = x.reshape(b, nc, chunk, G, hg, P)
    dt = dt.reshape(b, nc, chunk, G, hg)
    Bc = Bm.reshape(b, nc, chunk, G, D_STATE)
    Cc = Cm.reshape(b, nc, chunk, G, D_STATE)
    a_cum = jnp.cumsum(dt * A.reshape(G, hg), axis=2)
    dtx = dt[..., None] * x
    causal = jnp.tril(jnp.ones((chunk, chunk), dtype=bool))[None, None, :, :, None, None]
    seg = a_cum[:, :, :, None] - a_cum[:, :, None, :]
    decay = jnp.exp(jnp.where(causal, seg, -jnp.inf))
    cb = jnp.einsum('bctgn,bcsgn->bctsg', Cc, Bc)
    y_intra = jnp.einsum('bctsgh,bcsghp->bctghp', cb[..., None] * decay, dtx)
    decay_end = jnp.exp(a_cum[:, :, -1:] - a_cum)
    states = jnp.einsum('bcsgn,bcsghp->bcghpn', Bc, decay_end[..., None] * dtx)
    chunk_decay = jnp.exp(a_cum[:, :, -1])

    def step(h, inp):
        d, s = inp
        return h * d[..., None, None] + s, h

    hT, h_in = lax.scan(step, h0.reshape(b, G, hg, P, D_STATE),
                        (jnp.moveaxis(chunk_decay, 1, 0), jnp.moveaxis(states, 1, 0)))
    h_in = jnp.moveaxis(h_in, 0, 1)
    y_inter = jnp.einsum('bctgn,bcghpn->bctghp', Cc, h_in) * jnp.exp(a_cum)[..., None]
    y = (y_intra + y_inter).reshape(b, L, H, P)
    return y, hT.reshape(b, H, P, D_STATE)


def ssd_mixer(z, xbc, dt_raw, conv_buf, h0, conv_w, conv_b, dt_bias, a_log, d_skip, norm_g):
    b, L, _ = xbc.shape
    full = jnp.concatenate([conv_buf.astype(xbc.dtype), xbc], axis=1)
    conv = conv_b + full[:, 0:L] * conv_w[0]
    for k in range(1, CONV_W):
        conv = conv + full[:, k:k + L] * conv_w[k]
    new_buf = full[:, L:]
    xbc = jax.nn.silu(conv).astype(jnp.float32)
    xs = xbc[..., :W_B].reshape(b, L, H_B, SSM_HEAD_DIM)
    Bm = xbc[..., W_B:W_B + N_GROUPS * D_STATE].reshape(b, L, N_GROUPS, D_STATE)
    Cm = xbc[..., W_B + N_GROUPS * D_STATE:].reshape(b, L, N_GROUPS, D_STATE)
    dt = jax.nn.softplus(dt_raw.astype(jnp.float32) + dt_bias.astype(jnp.float32))
    A = -jnp.exp(a_log.astype(jnp.float32))
    chunk = SSD_CHUNK if L % SSD_CHUNK == 0 else L
    y, hT = ssd_scan(xs, dt, A, Bm, Cm, h0.astype(jnp.float32), chunk)
    y = y + d_skip.astype(jnp.float32)[:, None] * xs
    y = y.reshape(b, L, W_B) * jax.nn.silu(z.astype(jnp.float32))
    yg = y.reshape(b, L, N_GROUPS, W_B // N_GROUPS)
    yg = yg * lax.rsqrt(jnp.mean(yg * yg, axis=-1, keepdims=True) + EPS)
    y = yg.reshape(b, L, W_B) * norm_g.astype(jnp.float32)
    return y.astype(z.dtype), new_buf, hT


def decoder_layer(x, c, conv_buf, h0, w_ada, b_ada, norm_pre, norm_post, ffn_w_in, ffn_w_out,
                  w_in_mix, w_out_mix, gm_ln_g, gm_ln_b, gm_ws, gm_bs,
                  conv_w, conv_b, dt_bias, a_log, d_skip, ssm_norm_g):
    bsz = c.shape[0]
    mod = (jax.nn.silu(c) @ w_ada + b_ada).reshape(bsz, N_SUB, 3, D_MODEL)[:, :, :, None, :]

    def pre(h, i):
        return rmsnorm(h, norm_pre[i]) * (1 + mod[:, i, 1]) + mod[:, i, 0]

    def post(h, o, i, w):
        return h + w * mod[:, i, 2] * rmsnorm(o, norm_post[i])

    x = post(x, swiglu(pre(x, 0), ffn_w_in[0], ffn_w_out[0]), 0, FFN_RES)
    p = pre(x, 1) @ w_in_mix
    u, v, z, xbc, dt_raw = jnp.split(p, [W_A, 2 * W_A, 2 * W_A + W_B, 2 * W_A + W_B + CONV_DIM], axis=-1)
    ya, vn = gmlp_spatial_gate(u, v, gm_ln_g, gm_ln_b, gm_ws, gm_bs)
    yb, new_buf, hT = ssd_mixer(z, xbc, dt_raw, conv_buf, h0, conv_w, conv_b, dt_bias, a_log, d_skip, ssm_norm_g)
    x = post(x, jnp.concatenate([ya, yb.astype(ya.dtype)], axis=-1) @ w_out_mix, 1, 1.0)
    x = post(x, swiglu(pre(x, 2), ffn_w_in[1], ffn_w_out[1]), 2, FFN_RES)
    return x, new_buf, hT, vn


def setup_inputs(seed: int = 0) -> dict:
    key = jax.random.key(seed)
    ks = jax.random.split(key, 26)
    f32 = jnp.float32

    def nrm(k, shape, s):
        return jax.random.normal(k, shape, f32) * s

    dt0 = jnp.exp(jax.random.uniform(ks[20], (DEPTH, H_B), f32, np.log(1e-3), np.log(1e-1)))
    return {
        'x_prompt': nrm(ks[0], (BATCH, SEQ, D_MODEL), 1.0),
        'x_sample': nrm(ks[1], (DEC_BATCH, DEC_SEQ, D_MODEL), 1.0),
        'state_conv': nrm(ks[2], (DEPTH, DEC_BATCH, CONV_W - 1, CONV_DIM), 1.0),
        'state_ssm': nrm(ks[3], (DEPTH, DEC_BATCH, H_B, SSM_HEAD_DIM, D_STATE), 0.5),
        'c_prompt': nrm(ks[4], (BATCH, D_MODEL), 1.0),
        'c_sample': nrm(ks[5], (DEC_BATCH, D_MODEL), 1.0),
        'w_ada': nrm(ks[6], (DEPTH, D_MODEL, 3 * N_SUB * D_MODEL), 0.5 * D_MODEL ** -0.5),
        'b_ada': nrm(ks[7], (DEPTH, 3 * N_SUB * D_MODEL), 0.02),
        'norm_pre': 1.0 + nrm(ks[8], (DEPTH, N_SUB, D_MODEL), 0.05),
        'norm_post': 1.0 + nrm(ks[9], (DEPTH, N_SUB, D_MODEL), 0.05),
        'ffn_w_in': nrm(ks[10], (DEPTH, 2, D_MODEL, 2 * D_FF), D_MODEL ** -0.5),
        'ffn_w_out': nrm(ks[11], (DEPTH, 2, D_FF, D_MODEL), D_FF ** -0.5),
        'w_in_mix': nrm(ks[12], (DEPTH, D_MODEL, IN_COLS), D_MODEL ** -0.5),
        'w_out_mix': nrm(ks[13], (DEPTH, D_MIX, D_MODEL), D_MIX ** -0.5),
        'gm_ln_g': 1.0 + nrm(ks[14], (DEPTH, W_A), 0.05),
        'gm_ln_b': nrm(ks[15], (DEPTH, W_A), 0.02),
        'gm_ws': nrm(ks[16], (DEPTH, H_A, GM_CHUNK, GM_CHUNK), 0.5 * GM_CHUNK ** -0.5),
        'gm_bs': 1.0 + nrm(ks[17], (DEPTH, H_A, GM_CHUNK), 0.05),
        'conv_w': nrm(ks[18], (DEPTH, CONV_W, CONV_DIM), CONV_W ** -0.5),
        'conv_b': nrm(ks[19], (DEPTH, CONV_DIM), 0.02),
        'dt_bias': dt0 + jnp.log(-jnp.expm1(-dt0)),
        'a_log': jnp.log(jax.random.uniform(ks[21], (DEPTH, H_B), f32, 1.0, 16.0)),
        'd_skip': 1.0 + nrm(ks[22], (DEPTH, H_B), 0.05),
        'ssm_norm_g': 1.0 + nrm(ks[23], (DEPTH, W_B), 0.05),
    }


def reference(x_prompt, x_sample, state_conv, state_ssm, c_prompt, c_sample, w_ada, b_ada,
              norm_pre, norm_post, ffn_w_in, ffn_w_out, w_in_mix, w_out_mix, gm_ln_g, gm_ln_b,
              gm_ws, gm_bs, conv_w, conv_b, dt_bias, a_log, d_skip, ssm_norm_g):
    yp, ys = x_prompt, x_sample
    bp = x_prompt.shape[0]
    zero_buf = jnp.zeros((bp, CONV_W - 1, CONV_DIM), x_prompt.dtype)
    zero_h = jnp.zeros((bp, H_B, SSM_HEAD_DIM, D_STATE), jnp.float32)
    p_conv, p_ssm, s_conv, s_ssm, s_v = [], [], [], [], []
    for l in range(DEPTH):
        w = (w_ada[l], b_ada[l], norm_pre[l], norm_post[l], ffn_w_in[l], ffn_w_out[l],
             w_in_mix[l], w_out_mix[l], gm_ln_g[l], gm_ln_b[l], gm_ws[l], gm_bs[l],
             conv_w[l], conv_b[l], dt_bias[l], a_log[l], d_skip[l], ssm_norm_g[l])
        yp, buf_p, h_p, _ = decoder_layer(yp, c_prompt, zero_buf, zero_h, *w)
        ys, buf_s, h_s, v_s = decoder_layer(ys, c_sample, state_conv[l], state_ssm[l], *w)
        p_conv.append(buf_p)
        p_ssm.append(h_p)
        s_conv.append(buf_s)
        s_ssm.append(h_s)
        s_v.append(v_s)
    return (yp, ys, jnp.stack(p_conv), jnp.stack(p_ssm), jnp.stack(s_conv), jnp.stack(s_ssm), jnp.stack(s_v))
```

```python
import functools

import jax
import jax.numpy as jnp
from jax import lax
from jax.experimental import pallas as pl
from jax.experimental.pallas import tpu as pltpu

F32 = jnp.float32
BF16 = jnp.bfloat16
HIGHEST = lax.Precision.HIGHEST

GM_CHUNK = 128
SSD_CHUNK = 128
HEAD_P = 64
D_STATE = 128
N_GROUPS = 8
CONV_W = 4
N_SUB = 3
FFN_RES = 0.5
EPS = 1e-6

LANE = 128
SUBLANE = 8
BF16_SUBLANE = 16
VMEM_LIMIT_BYTES = 56 * 2**20

NT_DIMS = (((1,), (1,)), ((), ()))
TN_DIMS = (((0,), (0,)), ((), ()))


def _params(*sem):
    return pltpu.CompilerParams(dimension_semantics=sem, vmem_limit_bytes=VMEM_LIMIT_BYTES)


def _largest_divisor(n, cap, mult):
    for d in range(min(cap, n), 0, -1):
        if n % d == 0 and d % mult == 0:
            return d
    raise ValueError(f"no tile for {n} (cap {cap}, multiple of {mult})")


def _silu(x):
    return x * jax.nn.sigmoid(x)


def _rms(x):
    return x * lax.rsqrt(jnp.mean(x * x, axis=-1, keepdims=True) + EPS)


def _ada_kernel(c_ref, w_ref, b_ref, o_ref):
    a = _silu(c_ref[...]).astype(BF16)
    o_ref[...] = jnp.dot(a, w_ref[...].astype(BF16), preferred_element_type=F32) + b_ref[...]


def _ada(c_all, w_ada, b_ada):
    depth, d, n = w_ada.shape
    mc = c_all.shape[0]
    bn = _largest_divisor(n, 512, LANE)
    return pl.pallas_call(
        _ada_kernel,
        grid=(depth, n // bn),
        in_specs=[
            pl.BlockSpec((mc, d), lambda l, j: (0, 0)),
            pl.BlockSpec((None, d, bn), lambda l, j: (l, 0, j)),
            pl.BlockSpec((None, 1, bn), lambda l, j: (l, 0, j)),
        ],
        out_specs=pl.BlockSpec((None, mc, bn), lambda l, j: (l, 0, j)),
        out_shape=jax.ShapeDtypeStruct((depth, mc, n), F32),
        compiler_params=_params("arbitrary", "arbitrary"),
        name="ada_mod",
    )(c_all, w_ada, b_ada.reshape(depth, 1, n))


def _norm_mod(x, g, scale, shift):
    return _rms(x) * g * (1.0 + scale) + shift


def _pre_kernel(x_ref, g_ref, scp_ref, shp_ref, scs_ref, shs_ref, h_ref, *, n_ptiles, bs):
    i = pl.program_id(0)

    @pl.when(i < n_ptiles)
    def _():
        h_ref[...] = _norm_mod(x_ref[...], g_ref[...], scp_ref[...], shp_ref[...]).astype(BF16)

    @pl.when(i >= n_ptiles)
    def _():
        for r in range(x_ref.shape[0] // bs):
            rows = pl.ds(r * bs, bs)
            h_ref[rows, :] = _norm_mod(x_ref[rows, :], g_ref[...], scs_ref[...], shs_ref[...]).astype(BF16)


def _post_kernel(x_ref, o_ref, gpost_ref, gtp_ref, gts_ref, *rest, res_w, n_ptiles, bs, with_next):
    if with_next:
        gpre_ref, scp_ref, shp_ref, scs_ref, shs_ref, xo_ref, h_ref = rest
    else:
        (xo_ref,) = rest
    i = pl.program_id(0)

    def update(rows, gate, scale, shift):
        xn = x_ref[rows, :] + res_w * gate * (_rms(o_ref[rows, :]) * gpost_ref[...])
        xo_ref[rows, :] = xn
        if with_next:
            h_ref[rows, :] = _norm_mod(xn, gpre_ref[...], scale, shift).astype(BF16)

    @pl.when(i < n_ptiles)
    def _():
        if with_next:
            update(slice(None), gtp_ref[...], scp_ref[...], shp_ref[...])
        else:
            update(slice(None), gtp_ref[...], None, None)

    @pl.when(i >= n_ptiles)
    def _():
        for r in range(x_ref.shape[0] // bs):
            rows = pl.ds(r * bs, bs)
            if with_next:
                update(rows, gts_ref[...], scs_ref[...], shs_ref[...])
            else:
                update(rows, gts_ref[...], None, None)


class _Rows:
    def __init__(self, bp, seq, bs, t, d):
        self.bp, self.seq, self.bs, self.t, self.d = bp, seq, bs, t, d
        self.tp, self.ts = bp * seq, bs * t
        self.m = self.tp + self.ts
        ks = [k for k in range(t, 0, -1) if t % k == 0 and seq % (k * bs) == 0 and (k * bs <= 256 or k == 1)]
        if not ks:
            raise ValueError("prompt length must be a multiple of the sample batch")
        self.bm = ks[0] * bs
        self.n_ptiles = self.tp // self.bm
        self.n_tiles = self.m // self.bm
        self.tiles_per_seq = seq // self.bm

    def row_spec(self):
        return pl.BlockSpec((self.bm, self.d), lambda i: (i, 0))

    def vec_spec(self, idx):
        return pl.BlockSpec((None, 1, self.d), lambda i: (idx, 0, 0))

    def modp_spec(self, layer, k):
        bp, tps = self.bp, self.tiles_per_seq
        return pl.BlockSpec(
            (None, 1, self.d),
            lambda i: ((layer * bp + jnp.minimum(i // tps, bp - 1)) * (3 * N_SUB) + k, 0, 0))

    def mods_spec(self, layer, k):
        return pl.BlockSpec((None, self.bs, self.d), lambda i: (layer, 0, k))


def _pre(rows, x, norm_pre, mod_p, mod, layer, sub):
    kern = functools.partial(_pre_kernel, n_ptiles=rows.n_ptiles, bs=rows.bs)
    k_shift, k_scale = sub * 3, sub * 3 + 1
    return pl.pallas_call(
        kern,
        grid=(rows.n_tiles,),
        in_specs=[
            rows.row_spec(),
            rows.vec_spec(layer * N_SUB + sub),
            rows.modp_spec(layer, k_scale), rows.modp_spec(layer, k_shift),
            rows.mods_spec(layer, k_scale), rows.mods_spec(layer, k_shift),
        ],
        out_specs=rows.row_spec(),
        out_shape=jax.ShapeDtypeStruct((rows.m, rows.d), BF16),
        compiler_params=_params("arbitrary"),
        name="pre_norm",
    )(x, norm_pre, mod_p, mod_p, mod, mod)


def _post(rows, x, o, norm_pre, norm_post, mod_p, mod, layer, sub, res_w, nxt):
    with_next = nxt is not None
    kern = functools.partial(_post_kernel, res_w=res_w, n_ptiles=rows.n_ptiles, bs=rows.bs, with_next=with_next)
    k_gate = sub * 3 + 2
    in_specs = [
        rows.row_spec(), rows.row_spec(),
        rows.vec_spec(layer * N_SUB + sub),
        rows.modp_spec(layer, k_gate), rows.mods_spec(layer, k_gate),
    ]
    args = [x, o, norm_post, mod_p, mod]
    out_specs = [rows.row_spec()]
    out_shape = [jax.ShapeDtypeStruct((rows.m, rows.d), F32)]
    if with_next:
        nl, ns = nxt
        in_specs += [
            rows.vec_spec(nl * N_SUB + ns),
            rows.modp_spec(nl, ns * 3 + 1), rows.modp_spec(nl, ns * 3),
            rows.mods_spec(nl, ns * 3 + 1), rows.mods_spec(nl, ns * 3),
        ]
        args += [norm_pre, mod_p, mod_p, mod, mod]
        out_specs.append(rows.row_spec())
        out_shape.append(jax.ShapeDtypeStruct((rows.m, rows.d), BF16))
    res = pl.pallas_call(
        kern,
        grid=(rows.n_tiles,),
        in_specs=in_specs,
        out_specs=out_specs,
        out_shape=out_shape,
        compiler_params=_params("arbitrary"),
        name="post_residual",
    )(*args)
    return (res[0], res[1]) if with_next else (res[0], None)


def _mm_kernel(x_ref, w_ref, o_ref, *, nk):
    acc = jnp.dot(x_ref[...], w_ref[...], preferred_element_type=F32)
    if nk == 1:
        o_ref[...] = acc.astype(o_ref.dtype)
    else:
        k = pl.program_id(2)

        @pl.when(k == 0)
        def _():
            o_ref[...] = acc

        @pl.when(k > 0)
        def _():
            o_ref[...] += acc


def _mm(x, w, *, bn_cap, bk_cap):
    m, kd = x.shape
    n = w.shape[1]
    bm = _largest_divisor(m, 1088, BF16_SUBLANE)
    bn = _largest_divisor(n, bn_cap, LANE)
    bk = _largest_divisor(kd, bk_cap, LANE)
    nk = kd // bk
    return pl.pallas_call(
        functools.partial(_mm_kernel, nk=nk),
        grid=(m // bm, n // bn, nk),
        in_specs=[
            pl.BlockSpec((bm, bk), lambda i, j, k: (i, k)),
            pl.BlockSpec((bk, bn), lambda i, j, k: (k, j)),
        ],
        out_specs=pl.BlockSpec((bm, bn), lambda i, j, k: (i, j)),
        out_shape=jax.ShapeDtypeStruct((m, n), F32),
        compiler_params=_params("arbitrary", "arbitrary", "arbitrary"),
        name="matmul",
    )(x, w)


def _glu_kernel(x_ref, wg_ref, wu_ref, o_ref):
    x = x_ref[...]
    g = jnp.dot(x, wg_ref[...], preferred_element_type=F32)
    u = jnp.dot(x, wu_ref[...], preferred_element_type=F32)
    o_ref[...] = (_silu(g) * u).astype(BF16)


def _glu(x, w_gu, fp):
    m, kd = x.shape
    bm = _largest_divisor(m, 1088, BF16_SUBLANE)
    bn = _largest_divisor(fp, 512, LANE)
    nj = fp // bn
    return pl.pallas_call(
        _glu_kernel,
        grid=(m // bm, nj),
        in_specs=[
            pl.BlockSpec((bm, kd), lambda i, j: (i, 0)),
            pl.BlockSpec((kd, bn), lambda i, j: (0, j)),
            pl.BlockSpec((kd, bn), lambda i, j: (0, nj + j)),
        ],
        out_specs=pl.BlockSpec((bm, bn), lambda i, j: (i, j)),
        out_shape=jax.ShapeDtypeStruct((m, fp), BF16),
        compiler_params=_params("arbitrary", "arbitrary"),
        name="ffn_in_glu",
    )(x, w_gu, w_gu)


def _gelu_ln(v, ln_g, ln_b):
    gv = jax.nn.gelu(v)
    mu = jnp.mean(gv, axis=-1, keepdims=True)
    cen = gv - mu
    var = jnp.mean(cen * cen, axis=-1, keepdims=True)
    return cen * lax.rsqrt(var + EPS) * ln_g + ln_b


def _gmlp_p_kernel(u_ref, v_ref, lng_ref, lnb_ref, ws_ref, bs_ref, o_ref, *, n_heads):
    c = GM_CHUNK
    vn = _gelu_ln(v_ref[...], lng_ref[...], lnb_ref[...]).astype(BF16)
    tril = lax.broadcasted_iota(jnp.int32, (c, c), 0) >= lax.broadcasted_iota(jnp.int32, (c, c), 1)
    for h in range(n_heads):
        cols = slice(h * LANE, (h + 1) * LANE)
        w = jnp.where(tril, ws_ref[h], 0.0).astype(BF16)
        f = jnp.dot(w, vn[:, cols], preferred_element_type=F32) + bs_ref[:, cols]
        o_ref[:, cols] = (jax.nn.gelu(u_ref[:, cols]) * f).astype(BF16)


def _gmlp_prompt(p_uvz, tp, d, ln_g, ln_b, ws, bs_exp):
    n_heads = d // LANE
    c = GM_CHUNK
    return pl.pallas_call(
        functools.partial(_gmlp_p_kernel, n_heads=n_heads),
        grid=(tp // c,),
        in_specs=[
            pl.BlockSpec((c, d), lambda i: (i, 0)),
            pl.BlockSpec((c, d), lambda i: (i, 1)),
            pl.BlockSpec((1, d), lambda i: (0, 0)),
            pl.BlockSpec((1, d), lambda i: (0, 0)),
            pl.BlockSpec((n_heads, c, c), lambda i: (0, 0, 0)),
            pl.BlockSpec((c, d), lambda i: (0, 0)),
        ],
        out_specs=pl.BlockSpec((c, d), lambda i: (i, 0)),
        out_shape=jax.ShapeDtypeStruct((tp, d), BF16),
        compiler_params=_params("arbitrary"),
        name="gmlp_prompt",
    )(p_uvz, p_uvz, ln_g, ln_b, ws, bs_exp)


def _gmlp_s_kernel(u_ref, v_ref, lng_ref, lnb_ref, wsx_ref, bsx_ref, ya_ref, vn_ref, vnb, facc, *, t_len):
    t = pl.program_id(0)
    vn = _gelu_ln(v_ref[...], lng_ref[...], lnb_ref[...])
    vn_ref[...] = vn
    vnb[t] = vn.astype(BF16)
    facc[...] = jnp.broadcast_to(bsx_ref[t], facc.shape)
    w_t = wsx_ref[t]
    for s in range(t_len):
        @pl.when(s <= t)
        def _():
            w = w_t[s:s + 1, :].astype(BF16).astype(F32)
            facc[...] += w * vnb[s].astype(F32)
    ya_ref[...] = (jax.nn.gelu(u_ref[...]) * facc[...]).astype(BF16)


def _gmlp_sample(uvz3, d, ln_g, ln_b, wsx, bsx):
    t_len, bs, _ = uvz3.shape
    return pl.pallas_call(
        functools.partial(_gmlp_s_kernel, t_len=t_len),
        grid=(t_len,),
        in_specs=[
            pl.BlockSpec((None, bs, d), lambda t: (t, 0, 0)),
            pl.BlockSpec((None, bs, d), lambda t: (t, 0, 1)),
            pl.BlockSpec((1, d), lambda t: (0, 0)),
            pl.BlockSpec((1, d), lambda t: (0, 0)),
            pl.BlockSpec((t_len, t_len, d), lambda t: (0, 0, 0)),
            pl.BlockSpec((t_len, 1, d), lambda t: (0, 0, 0)),
        ],
        out_specs=[
            pl.BlockSpec((None, bs, d), lambda t: (t, 0, 0)),
            pl.BlockSpec((None, bs, d), lambda t: (t, 0, 0)),
        ],
        out_shape=[
            jax.ShapeDtypeStruct((t_len, bs, d), BF16),
            jax.ShapeDtypeStruct((t_len, bs, d), F32),
        ],
        scratch_shapes=[pltpu.VMEM((t_len, bs, d), BF16), pltpu.VMEM((bs, d), F32)],
        compiler_params=_params("arbitrary"),
        name="gmlp_sample",
    )(uvz3, uvz3, ln_g, ln_b, wsx, bsx)


def _head_expand(gw):
    row = lax.broadcasted_iota(jnp.int32, (LANE, gw), 0)
    col = lax.broadcasted_iota(jnp.int32, (LANE, gw), 1)
    return (lax.shift_right_logical(col, 6) == row).astype(F32)


def _gated_group_norm(y, z, ng):
    y = y * _silu(z)
    return (_rms(y) * ng).astype(BF16)


def _ssd_p_kernel(x_ref, b_ref, c_ref, z_ref, dt_ref, cwx_ref, cwb_ref, cwc_ref, cbx_ref, cbb_ref, cbc_ref,
                  dtb_ref, alog_ref, dskip_ref, ng_ref,
                  y_ref, cx_out, cb_out, cc_out, h_out,
                  xpad, bpad, cpad, state, ybuf, *, hpg):
    n = SSD_CHUNK
    tail = CONV_W - 1
    first = SUBLANE - tail
    ci = pl.program_id(2)

    @pl.when(ci == 0)
    def _():
        state[...] = jnp.zeros_like(state)
        for pad in (xpad, bpad, cpad):
            pad[pl.ds(0, SUBLANE), :] = jnp.zeros((SUBLANE, pad.shape[1]), F32)

    def conv_act(src_ref, pad_ref, cw_ref, cb_ref, out_ref):
        x = src_ref[...]
        pad_ref[pl.ds(SUBLANE, n), :] = x
        acc = cb_ref[...] + cw_ref[0:1, :] * pad_ref[pl.ds(first, n), :]
        for k in range(1, tail):
            acc = acc + cw_ref[k:k + 1, :] * pad_ref[pl.ds(first + k, n), :]
        acc = acc + cw_ref[tail:tail + 1, :] * x
        last = src_ref[pl.ds(n - tail, tail), :]
        pad_ref[pl.ds(first, tail), :] = last
        out_ref[...] = last
        return _silu(acc)

    xs = conv_act(x_ref, xpad, cwx_ref, cbx_ref, cx_out)
    bm = conv_act(b_ref, bpad, cwb_ref, cbb_ref, cb_out).astype(BF16)
    cm = conv_act(c_ref, cpad, cwc_ref, cbc_ref, cc_out).astype(BF16)

    gw = xs.shape[1]
    dt = jax.nn.softplus(dt_ref[...] + dtb_ref[...])
    a = dt * (-jnp.exp(alog_ref[...]))
    tril = lax.broadcasted_iota(jnp.int32, (n, n), 0) >= lax.broadcasted_iota(jnp.int32, (n, n), 1)
    a_cum = jnp.dot(tril.astype(F32), a, precision=HIGHEST, preferred_element_type=F32)
    a_cum_t = a_cum.T
    expand = _head_expand(gw)
    dt_e = jnp.dot(dt, expand, precision=HIGHEST, preferred_element_type=F32)
    acum_e = jnp.dot(a_cum, expand, precision=HIGHEST, preferred_element_type=F32)
    dtx = dt_e * xs
    w_end = (jnp.exp(acum_e[n - 1:n, :] - acum_e) * dtx).astype(BF16)
    dtx_b = dtx.astype(BF16)
    exp_acum = jnp.exp(acum_e)
    cb = lax.dot_general(cm, bm, NT_DIMS, preferred_element_type=F32)
    a_last = a_cum[n - 1:n, :]
    for j in range(hpg):
        cols = slice(j * HEAD_P, (j + 1) * HEAD_P)
        seg = a_cum[:, j:j + 1] - a_cum_t[j:j + 1, :]
        decay = jnp.exp(jnp.where(tril, seg, -jnp.inf))
        y_intra = jnp.dot((cb * decay).astype(BF16), dtx_b[:, cols], preferred_element_type=F32)
        h_in = state[j]
        y_inter = lax.dot_general(cm, h_in.astype(BF16), NT_DIMS, preferred_element_type=F32)
        ybuf[:, cols] = y_intra + y_inter * exp_acum[:, cols]
        new = lax.dot_general(w_end[:, cols], bm, TN_DIMS, preferred_element_type=F32)
        state[j] = h_in * jnp.exp(a_last[:, j:j + 1]) + new
    y = ybuf[...] + dskip_ref[...] * xs
    y_ref[...] = _gated_group_norm(y, z_ref[...], ng_ref[...])
    h_out[...] = state[...]


def _ssd_prompt(p_xbc, p_uvz, dtp, bp, seq, d, cw8, cb1, dtb_g, alog_g, dskip_e, ng):
    g = N_GROUPS
    gw = d // g
    hpg = gw // HEAD_P
    n = SSD_CHUNK
    nc = seq // n
    tail = CONV_W - 1
    b_off = d // D_STATE
    c_off = b_off + g
    z_off = 2 * d // gw

    def rows(width, off):
        return pl.BlockSpec((n, width), lambda b, gi, c: (b * nc + c, off + gi))

    def par(r, width, off):
        return pl.BlockSpec((r, width), lambda b, gi, c: (0, off + gi))

    def grp():
        return pl.BlockSpec((None, 1, LANE), lambda b, gi, c: (gi, 0, 0))

    def conv_out(width):
        return pl.BlockSpec((None, tail, width), lambda b, gi, c: (b, 0, gi))

    outs = pl.pallas_call(
        functools.partial(_ssd_p_kernel, hpg=hpg),
        grid=(bp, g, nc),
        in_specs=[
            rows(gw, 0), rows(D_STATE, b_off), rows(D_STATE, c_off), rows(gw, z_off), rows(LANE, 0),
            par(SUBLANE, gw, 0), par(SUBLANE, D_STATE, b_off), par(SUBLANE, D_STATE, c_off),
            par(1, gw, 0), par(1, D_STATE, b_off), par(1, D_STATE, c_off),
            grp(), grp(), par(1, gw, 0), par(1, gw, 0),
        ],
        out_specs=[
            pl.BlockSpec((n, gw), lambda b, gi, c: (b * nc + c, gi)),
            conv_out(gw), conv_out(D_STATE), conv_out(D_STATE),
            pl.BlockSpec((None, hpg, HEAD_P, D_STATE), lambda b, gi, c: (b, gi, 0, 0)),
        ],
        out_shape=[
            jax.ShapeDtypeStruct((bp * seq, d), BF16),
            jax.ShapeDtypeStruct((bp, tail, d), F32),
            jax.ShapeDtypeStruct((bp, tail, g * D_STATE), F32),
            jax.ShapeDtypeStruct((bp, tail, g * D_STATE), F32),
            jax.ShapeDtypeStruct((bp, g * hpg, HEAD_P, D_STATE), F32),
        ],
        scratch_shapes=[
            pltpu.VMEM((SUBLANE + n, gw), F32),
            pltpu.VMEM((SUBLANE + n, D_STATE), F32),
            pltpu.VMEM((SUBLANE + n, D_STATE), F32),
            pltpu.VMEM((hpg, HEAD_P, D_STATE), F32),
            pltpu.VMEM((n, gw), F32),
        ],
        compiler_params=_params("arbitrary", "arbitrary", "arbitrary"),
        name="ssd_prompt",
    )(p_xbc, p_xbc, p_xbc, p_uvz, dtp, cw8, cw8, cw8, cb1, cb1, cb1, dtb_g, alog_g, dskip_e, ng)
    yb, cx, cb_, cc, h_t = outs
    return yb, jnp.concatenate([cx, cb_, cc], axis=-1), h_t


def _ssd_s_kernel(x_ref, b_ref, c_ref, z_ref, dt_ref, sx_ref, sb_ref, sc_ref,
                  cwx_ref, cwb_ref, cwc_ref, cbx_ref, cbb_ref, cbc_ref,
                  dtb_ref, alog_ref, dskip_ref, ng_ref, h0_ref,
                  y_ref, cx_out, cb_out, cc_out, h_out,
                  cbuf, bbuf, wbuf, yibuf, *, t_len, sb):
    tail = CONV_W - 1

    def conv_act(src_ref, st_ref, cw_ref, cb_ref, out_ref):
        full = [st_ref[r] for r in range(tail)] + [src_ref[t] for t in range(t_len)]
        acts = []
        for t in range(t_len):
            acc = cb_ref[...] + cw_ref[0:1, :] * full[t]
            for k in range(1, CONV_W):
                acc = acc + cw_ref[k:k + 1, :] * full[t + k]
            acts.append(_silu(acc))
        for r in range(tail):
            out_ref[r] = full[t_len + r]
        return acts

    xs = conv_act(x_ref, sx_ref, cwx_ref, cbx_ref, cx_out)
    bm = conv_act(b_ref, sb_ref, cwb_ref, cbb_ref, cb_out)
    cm = conv_act(c_ref, sc_ref, cwc_ref, cbc_ref, cc_out)
    gw = xs[0].shape[1]

    a_neg = -jnp.exp(alog_ref[...])
    dts, acums = [], []
    run = None
    for t in range(t_len):
        dt = jax.nn.softplus(dt_ref[t] + dtb_ref[...])
        run = dt * a_neg if run is None else run + dt * a_neg
        dts.append(dt)
        acums.append(run)
    expand = _head_expand(gw)
    dt_e = jnp.dot(jnp.concatenate(dts, axis=0), expand, precision=HIGHEST, preferred_element_type=F32)
    acum_all = jnp.dot(jnp.concatenate(acums, axis=0), expand, precision=HIGHEST, preferred_element_type=F32)
    acum_e = [acum_all[t * sb:(t + 1) * sb] for t in range(t_len)]
    dtx = [dt_e[t * sb:(t + 1) * sb] * xs[t] for t in range(t_len)]
    a_last = acum_e[t_len - 1]

    n_panels = gw // LANE

    def put_rows(buf, row0, val):
        for p in range(n_panels):
            buf[p, pl.ds(row0, sb), :] = val[:, p * LANE:(p + 1) * LANE]

    def seq_rows(buf, q):
        return jnp.concatenate([buf[p, pl.ds(q, SUBLANE, stride=sb), :] for p in range(n_panels)], axis=1)

    zero_rows = (SUBLANE - t_len) * sb
    for buf in (cbuf, bbuf):
        buf[pl.ds(t_len * sb, zero_rows), :] = jnp.zeros((zero_rows, D_STATE), F32)
    wbuf[:, pl.ds(t_len * sb, zero_rows), :] = jnp.zeros((n_panels, zero_rows, LANE), F32)
    for t in range(t_len):
        cbuf[pl.ds(t * sb, sb), :] = cm[t]
        bbuf[pl.ds(t * sb, sb), :] = bm[t]
        put_rows(wbuf, t * sb, jnp.exp(a_last - acum_e[t]) * dtx[t])
    chunk_decay = jnp.exp(a_last)
    ones = jnp.ones((SUBLANE, D_STATE), BF16)
    for q in range(sb):
        c_q = cbuf[pl.ds(q, SUBLANE, stride=sb), :].astype(BF16)
        b_q = bbuf[pl.ds(q, SUBLANE, stride=sb), :].astype(BF16)
        w_q = seq_rows(wbuf, q).astype(BF16)
        h0 = h0_ref[q]
        y_q = lax.dot_general(c_q, h0.astype(BF16), NT_DIMS, preferred_element_type=F32)
        for p in range(n_panels):
            yibuf[p, pl.ds(q, SUBLANE, stride=sb), :] = y_q[:, p * LANE:(p + 1) * LANE]
        new = lax.dot_general(w_q, b_q, TN_DIMS, preferred_element_type=F32)
        cd = chunk_decay[q:q + 1, :]
        hi = cd.astype(BF16).astype(F32)
        cd8 = jnp.concatenate([hi, cd - hi, jnp.zeros((SUBLANE - 2, gw), F32)], axis=0).astype(BF16)
        cd_col = lax.dot_general(cd8, ones, TN_DIMS, preferred_element_type=F32)
        h_out[q] = h0 * cd_col + new

    for t in range(t_len):
        y = None
        for s in range(t + 1):
            cbts = jnp.sum(cm[t] * bm[s], axis=-1, keepdims=True)
            term = cbts * jnp.exp(acum_e[t] - acum_e[s]) * dtx[s]
            y = term if y is None else y + term
        y_inter = jnp.concatenate([yibuf[p, pl.ds(t * sb, sb), :] for p in range(n_panels)], axis=1)
        y = y + y_inter * jnp.exp(acum_e[t]) + dskip_ref[...] * xs[t]
        y_ref[t] = _gated_group_norm(y, z_ref[t], ng_ref[...])


def _ssd_sample(xbc3, uvz3, dt3, conv3, state_ssm4, layer, d, cw8, cb1, dtb_g, alog_g, dskip_e, ng):
    t_len, bs, _ = xbc3.shape
    assert t_len <= SUBLANE
    g = N_GROUPS
    gw = d // g
    tail = CONV_W - 1
    sb = SUBLANE
    b_off = d // D_STATE
    c_off = b_off + g
    z_off = 2 * d // gw

    def slab(r, width, off):
        return pl.BlockSpec((r, sb, width), lambda i, gi: (0, i, off + gi))

    def par(r, width, off):
        return pl.BlockSpec((r, width), lambda i, gi: (0, off + gi))

    def grp():
        return pl.BlockSpec((None, 1, LANE), lambda i, gi: (gi, 0, 0))

    outs = pl.pallas_call(
        functools.partial(_ssd_s_kernel, t_len=t_len, sb=sb),
        grid=(bs // sb, g),
        in_specs=[
            slab(t_len, gw, 0), slab(t_len, D_STATE, b_off), slab(t_len, D_STATE, c_off),
            slab(t_len, gw, z_off), slab(t_len, LANE, 0),
            slab(tail, gw, 0), slab(tail, D_STATE, b_off), slab(tail, D_STATE, c_off),
            par(SUBLANE, gw, 0), par(SUBLANE, D_STATE, b_off), par(SUBLANE, D_STATE, c_off),
            par(1, gw, 0), par(1, D_STATE, b_off), par(1, D_STATE, c_off),
            grp(), grp(), par(1, gw, 0), par(1, gw, 0),
            pl.BlockSpec((None, sb, gw, D_STATE), lambda i, gi: (layer, i, gi, 0)),
        ],
        out_specs=[
            slab(t_len, gw, 0), slab(tail, gw, 0), slab(tail, D_STATE, 0), slab(tail, D_STATE, 0),
            pl.BlockSpec((sb, gw, D_STATE), lambda i, gi: (i, gi, 0)),
        ],
        out_shape=[
            jax.ShapeDtypeStruct((t_len, bs, d), BF16),
            jax.ShapeDtypeStruct((tail, bs, d), F32),
            jax.ShapeDtypeStruct((tail, bs, g * D_STATE), F32),
            jax.ShapeDtypeStruct((tail, bs, g * D_STATE), F32),
            jax.ShapeDtypeStruct((bs, d, D_STATE), F32),
        ],
        scratch_shapes=[
            pltpu.VMEM((SUBLANE * sb, D_STATE), F32),
            pltpu.VMEM((SUBLANE * sb, D_STATE), F32),
            pltpu.VMEM((gw // LANE, SUBLANE * sb, LANE), F32),
            pltpu.VMEM((gw // LANE, SUBLANE * sb, LANE), F32),
        ],
        compiler_params=_params("arbitrary", "arbitrary"),
        name="ssd_sample",
    )(xbc3, xbc3, xbc3, uvz3, dt3, conv3, conv3, conv3, cw8, cw8, cw8, cb1, cb1, cb1,
      dtb_g, alog_g, dskip_e, ng, state_ssm4)
    yb, cx, cb_, cc, h_t = outs
    return yb, jnp.concatenate([cx, cb_, cc], axis=-1), h_t


def _group_pad(v):
    hpg = v.shape[0] // N_GROUPS
    return jnp.pad(v.reshape(N_GROUPS, hpg), ((0, 0), (0, LANE - hpg))).reshape(N_GROUPS, 1, LANE)


def kernel(x_prompt, x_sample, state_conv, state_ssm, c_prompt, c_sample, w_ada, b_ada, norm_pre, norm_post,
           ffn_w_in, ffn_w_out, w_in_mix, w_out_mix, gm_ln_g, gm_ln_b, gm_ws, gm_bs, conv_w, conv_b,
           dt_bias, a_log, d_skip, ssm_norm_g):
    bp, seq, d = x_prompt.shape
    bs, t_len, _ = x_sample.shape
    depth = w_ada.shape[0]
    f = ffn_w_out.shape[2]
    h_b = dt_bias.shape[1]
    conv_dim = conv_w.shape[2]
    g = N_GROUPS
    hpg = h_b // g
    tail = CONV_W - 1
    rows = _Rows(bp, seq, bs, t_len, d)
    tp = rows.tp
    fp = -(-f // 512) * 512

    x = jnp.concatenate([x_prompt.reshape(tp, d), x_sample.transpose(1, 0, 2).reshape(rows.ts, d)], axis=0)
    mc = -(-(bs + bp) // BF16_SUBLANE) * BF16_SUBLANE
    c_all = jnp.concatenate([c_sample, c_prompt, jnp.zeros((mc - bs - bp, d), F32)], axis=0)
    mod = _ada(c_all, w_ada, b_ada)
    mod_p = mod[:, bs:bs + bp].reshape(depth * bp * 3 * N_SUB, 1, d)
    npre = norm_pre.reshape(depth * N_SUB, 1, d)
    npost = norm_post.reshape(depth * N_SUB, 1, d)
    state_ssm4 = state_ssm.reshape(depth, bs, h_b * HEAD_P, D_STATE)

    def ffn(h, layer, i):
        w_in = ffn_w_in[layer, i]
        pad = ((0, 0), (0, fp - f))
        w_gu = jnp.concatenate([jnp.pad(w_in[:, :f].astype(BF16), pad), jnp.pad(w_in[:, f:].astype(BF16), pad)], axis=1)
        w_out = jnp.pad(ffn_w_out[layer, i].astype(BF16), ((0, fp - f), (0, 0)))
        return _mm(_glu(h, w_gu, fp), w_out, bn_cap=2048, bk_cap=1024)

    outs = {k: [] for k in ("p_conv", "p_ssm", "s_conv", "s_ssm", "s_v")}
    h = _pre(rows, x, npre, mod_p, mod, 0, 0)
    for layer in range(depth):
        x, h = _post(rows, x, ffn(h, layer, 0), npre, npost, mod_p, mod, layer, 0, FFN_RES, (layer, 1))

        w_mix = w_in_mix[layer]
        w_dt = w_mix[:, 3 * d + conv_dim:].reshape(d, g, hpg)
        w_dtp = jnp.pad(w_dt, ((0, 0), (0, 0), (0, LANE - hpg))).reshape(d, g * LANE).astype(BF16)
        p_uvz = _mm(h, w_mix[:, :3 * d].astype(BF16), bn_cap=1024, bk_cap=d)
        p_xbc = _mm(h, w_mix[:, 3 * d:3 * d + conv_dim].astype(BF16), bn_cap=1024, bk_cap=d)
        dtp = _mm(h, w_dtp, bn_cap=1024, bk_cap=d)

        ln_g = gm_ln_g[layer].reshape(1, d)
        ln_b = gm_ln_b[layer].reshape(1, d)
        bs_exp = jnp.repeat(gm_bs[layer].T, LANE, axis=1)
        ya_p = _gmlp_prompt(p_uvz, tp, d, ln_g, ln_b, gm_ws[layer], bs_exp)
        uvz3 = p_uvz[tp:].reshape(t_len, bs, 3 * d)
        wsx = jnp.repeat(gm_ws[layer][:, :t_len, :t_len].transpose(1, 2, 0), LANE, axis=2)
        bsx = jnp.repeat(gm_bs[layer][:, :t_len].T, LANE, axis=1).reshape(t_len, 1, d)
        ya_s, vn_s = _gmlp_sample(uvz3, d, ln_g, ln_b, wsx, bsx)

        cw8 = jnp.pad(conv_w[layer], ((0, SUBLANE - CONV_W), (0, 0)))
        cb1 = conv_b[layer].reshape(1, conv_dim)
        dtb_g, alog_g = _group_pad(dt_bias[layer]), _group_pad(a_log[layer])
        dskip_e = jnp.repeat(d_skip[layer], HEAD_P).reshape(1, d)
        ng = ssm_norm_g[layer].reshape(1, d)
        yb_p, conv_p, ssm_p = _ssd_prompt(p_xbc, p_uvz, dtp, bp, seq, d, cw8, cb1, dtb_g, alog_g, dskip_e, ng)
        xbc3 = p_xbc[tp:].reshape(t_len, bs, conv_dim)
        dt3 = dtp[tp:].reshape(t_len, bs, g * LANE)
        conv3 = state_conv[layer].transpose(1, 0, 2)
        yb_s, conv_s, ssm_s = _ssd_sample(xbc3, uvz3, dt3, conv3, state_ssm4, layer, d,
                                          cw8, cb1, dtb_g, alog_g, dskip_e, ng)

        y_mix = jnp.concatenate([
            jnp.concatenate([ya_p, yb_p], axis=1),
            jnp.concatenate([ya_s.reshape(rows.ts, d), yb_s.reshape(rows.ts, d)], axis=1)], axis=0)
        o = _mm(y_mix, w_out_mix[layer].astype(BF16), bn_cap=2048, bk_cap=1024)
        x, h = _post(rows, x, o, npre, npost, mod_p, mod, layer, 1, 1.0, (layer, 2))

        nxt = (layer + 1, 0) if layer + 1 < depth else None
        x, h = _post(rows, x, ffn(h, layer, 1), npre, npost, mod_p, mod, layer, 2, FFN_RES, nxt)

        outs["p_conv"].append(conv_p)
        outs["p_ssm"].append(ssm_p)
        outs["s_conv"].append(conv_s.transpose(1, 0, 2))
        outs["s_ssm"].append(ssm_s.reshape(bs, h_b, HEAD_P, D_STATE))
        outs["s_v"].append(vn_s.transpose(1, 0, 2))

    y_prompt = x[:tp].reshape(bp, seq, d)
    y_sample = x[tp:].reshape(t_len, bs, d).transpose(1, 0, 2)
    return (y_prompt, y_sample, jnp.stack(outs["p_conv"]), jnp.stack(outs["p_ssm"]),
            jnp.stack(outs["s_conv"]), jnp.stack(outs["s_ssm"]), jnp.stack(outs["s_v"]))
```

```python
import functools

import jax
import jax.numpy as jnp
from jax import lax
from jax.experimental import pallas as pl
from jax.experimental.pallas import tpu as pltpu

F32 = jnp.float32
BF16 = jnp.bfloat16
HIGHEST = lax.Precision.HIGHEST

GM_CHUNK = 128
SSD_CHUNK = 128
HEAD_P = 64
D_STATE = 128
N_GROUPS = 8
CONV_W = 4
N_SUB = 3
FFN_RES = 0.5
EPS = 1e-6

LANE = 128
SUBLANE = 8
BF16_SUBLANE = 16
VMEM_LIMIT_BYTES = 56 * 2**20

NT_DIMS = (((1,), (1,)), ((), ()))
TN_DIMS = (((0,), (0,)), ((), ()))


def _params(*sem):
    return pltpu.CompilerParams(dimension_semantics=sem, vmem_limit_bytes=VMEM_LIMIT_BYTES)


def _largest_divisor(n, cap, mult):
    for d in range(min(cap, n), 0, -1):
        if n % d == 0 and d % mult == 0:
            return d
    raise ValueError(f"no tile for {n} (cap {cap}, multiple of {mult})")


def _silu(x):
    return x * jax.nn.sigmoid(x)


def _rms(x):
    return x * lax.rsqrt(jnp.mean(x * x, axis=-1, keepdims=True) + EPS)


def _ada_kernel(c_ref, w_ref, b_ref, o_ref):
    a = _silu(c_ref[...]).astype(BF16)
    o_ref[...] = jnp.dot(a, w_ref[...].astype(BF16), preferred_element_type=F32) + b_ref[...]


def _ada(c_all, w_ada, b_ada):
    depth, d, n = w_ada.shape
    mc = c_all.shape[0]
    bn = _largest_divisor(n, 512, LANE)
    return pl.pallas_call(
        _ada_kernel,
        grid=(depth, n // bn),
        in_specs=[
            pl.BlockSpec((mc, d), lambda l, j: (0, 0)),
            pl.BlockSpec((None, d, bn), lambda l, j: (l, 0, j)),
            pl.BlockSpec((None, 1, bn), lambda l, j: (l, 0, j)),
        ],
        out_specs=pl.BlockSpec((None, mc, bn), lambda l, j: (l, 0, j)),
        out_shape=jax.ShapeDtypeStruct((depth, mc, n), F32),
        compiler_params=_params("arbitrary", "arbitrary"),
        name="ada_mod",
    )(c_all, w_ada, b_ada.reshape(depth, 1, n))


def _norm_mod(x, g, scale, shift):
    return _rms(x) * g * (1.0 + scale) + shift


def _pre_kernel(x_ref, g_ref, scp_ref, shp_ref, scs_ref, shs_ref, h_ref, *, n_ptiles, bs):
    i = pl.program_id(0)

    @pl.when(i < n_ptiles)
    def _():
        h_ref[...] = _norm_mod(x_ref[...], g_ref[...], scp_ref[...], shp_ref[...]).astype(BF16)

    @pl.when(i >= n_ptiles)
    def _():
        for r in range(x_ref.shape[0] // bs):
            rows = pl.ds(r * bs, bs)
            h_ref[rows, :] = _norm_mod(x_ref[rows, :], g_ref[...], scs_ref[...], shs_ref[...]).astype(BF16)


def _post_kernel(x_ref, o_ref, gpost_ref, gtp_ref, gts_ref, *rest, res_w, n_ptiles, bs, with_next):
    if with_next:
        gpre_ref, scp_ref, shp_ref, scs_ref, shs_ref, xo_ref, h_ref = rest
    else:
        (xo_ref,) = rest
    i = pl.program_id(0)

    def update(rows, gate, scale, shift):
        xn = x_ref[rows, :] + res_w * gate * (_rms(o_ref[rows, :]) * gpost_ref[...])
        xo_ref[rows, :] = xn
        if with_next:
            h_ref[rows, :] = _norm_mod(xn, gpre_ref[...], scale, shift).astype(BF16)

    @pl.when(i < n_ptiles)
    def _():
        if with_next:
            update(slice(None), gtp_ref[...], scp_ref[...], shp_ref[...])
        else:
            update(slice(None), gtp_ref[...], None, None)

    @pl.when(i >= n_ptiles)
    def _():
        for r in range(x_ref.shape[0] // bs):
            rows = pl.ds(r * bs, bs)
            if with_next:
                update(rows, gts_ref[...], scs_ref[...], shs_ref[...])
            else:
                update(rows, gts_ref[...], None, None)


class _Rows:
    def __init__(self, bp, seq, bs, t, d):
        self.bp, self.seq, self.bs, self.t, self.d = bp, seq, bs, t, d
        self.tp, self.ts = bp * seq, bs * t
        self.m = self.tp + self.ts
        ks = [k for k in range(t, 0, -1) if t % k == 0 and seq % (k * bs) == 0 and (k * bs <= 256 or k == 1)]
        if not ks:
            raise ValueError("prompt length must be a multiple of the sample batch")
        self.bm = ks[0] * bs
        self.n_ptiles = self.tp // self.bm
        self.n_tiles = self.m // self.bm
        self.tiles_per_seq = seq // self.bm

    def row_spec(self):
        return pl.BlockSpec((self.bm, self.d), lambda i: (i, 0))

    def vec_spec(self, idx):
        return pl.BlockSpec((None, 1, self.d), lambda i: (idx, 0, 0))

    def modp_spec(self, layer, k):
        bp, tps = self.bp, self.tiles_per_seq
        return pl.BlockSpec(
            (None, 1, self.d),
            lambda i: ((layer * bp + jnp.minimum(i // tps, bp - 1)) * (3 * N_SUB) + k, 0, 0))

    def mods_spec(self, layer, k):
        return pl.BlockSpec((None, self.bs, self.d), lambda i: (layer, 0, k))


def _pre(rows, x, norm_pre, mod_p, mod, layer, sub):
    kern = functools.partial(_pre_kernel, n_ptiles=rows.n_ptiles, bs=rows.bs)
    k_shift, k_scale = sub * 3, sub * 3 + 1
    return pl.pallas_call(
        kern,
        grid=(rows.n_tiles,),
        in_specs=[
            rows.row_spec(),
            rows.vec_spec(layer * N_SUB + sub),
            rows.modp_spec(layer, k_scale), rows.modp_spec(layer, k_shift),
            rows.mods_spec(layer, k_scale), rows.mods_spec(layer, k_shift),
        ],
        out_specs=rows.row_spec(),
        out_shape=jax.ShapeDtypeStruct((rows.m, rows.d), BF16),
        compiler_params=_params("arbitrary"),
        name="pre_norm",
    )(x, norm_pre, mod_p, mod_p, mod, mod)


def _post(rows, x, o, norm_pre, norm_post, mod_p, mod, layer, sub, res_w, nxt):
    with_next = nxt is not None
    kern = functools.partial(_post_kernel, res_w=res_w, n_ptiles=rows.n_ptiles, bs=rows.bs, with_next=with_next)
    k_gate = sub * 3 + 2
    in_specs = [
        rows.row_spec(), rows.row_spec(),
        rows.vec_spec(layer * N_SUB + sub),
        rows.modp_spec(layer, k_gate), rows.mods_spec(layer, k_gate),
    ]
    args = [x, o, norm_post, mod_p, mod]
    out_specs = [rows.row_spec()]
    out_shape = [jax.ShapeDtypeStruct((rows.m, rows.d), F32)]
    if with_next:
        nl, ns = nxt
        in_specs += [
            rows.vec_spec(nl * N_SUB + ns),
            rows.modp_spec(nl, ns * 3 + 1), rows.modp_spec(nl, ns * 3),
            rows.mods_spec(nl, ns * 3 + 1), rows.mods_spec(nl, ns * 3),
        ]
        args += [norm_pre, mod_p, mod_p, mod, mod]
        out_specs.append(rows.row_spec())
        out_shape.append(jax.ShapeDtypeStruct((rows.m, rows.d), BF16))
    res = pl.pallas_call(
        kern,
        grid=(rows.n_tiles,),
        in_specs=in_specs,
        out_specs=out_specs,
        out_shape=out_shape,
        compiler_params=_params("arbitrary"),
        name="post_residual",
    )(*args)
    return (res[0], res[1]) if with_next else (res[0], None)


def _mm_acc_kernel(x_ref, w_ref, o_ref):
    k = pl.program_id(2)

    @pl.when(k == 0)
    def _():
        o_ref[...] = jnp.dot(x_ref[...], w_ref[...], preferred_element_type=F32)

    @pl.when(k > 0)
    def _():
        o_ref[...] += jnp.dot(x_ref[...], w_ref[...], preferred_element_type=F32)


def _mm_acc(x, w):
    m, kd = x.shape
    n = w.shape[1]
    bm = _largest_divisor(m, 1088, BF16_SUBLANE)
    bn = _largest_divisor(n, 2048, LANE)
    bk = _largest_divisor(kd, 1024, LANE)
    return pl.pallas_call(
        _mm_acc_kernel,
        grid=(m // bm, n // bn, kd // bk),
        in_specs=[
            pl.BlockSpec((bm, bk), lambda i, j, k: (i, k)),
            pl.BlockSpec((bk, bn), lambda i, j, k: (k, j)),
        ],
        out_specs=pl.BlockSpec((bm, bn), lambda i, j, k: (i, j)),
        out_shape=jax.ShapeDtypeStruct((m, n), F32),
        compiler_params=_params("arbitrary", "arbitrary", "arbitrary"),
        name="matmul_acc",
    )(x, w)


def _mm_w32_kernel(x_ref, w_ref, o_ref):
    o_ref[...] = jnp.dot(x_ref[...], w_ref[...].astype(BF16), preferred_element_type=F32)


def _row_block(m):
    return _largest_divisor(m, 2176, BF16_SUBLANE)


def _mm_w32(x, w_stack, layer, n_cols, *, bn=512):
    m, kd = x.shape
    bm = _row_block(m)
    bn = _largest_divisor(n_cols, bn, LANE)
    return pl.pallas_call(
        _mm_w32_kernel,
        grid=(m // bm, n_cols // bn),
        in_specs=[
            pl.BlockSpec((bm, kd), lambda i, j: (i, 0), pipeline_mode=pl.Buffered(1)),
            pl.BlockSpec((None, kd, bn), lambda i, j: (layer, 0, j)),
        ],
        out_specs=pl.BlockSpec((bm, bn), lambda i, j: (i, j)),
        out_shape=jax.ShapeDtypeStruct((m, n_cols), F32),
        compiler_params=_params("arbitrary", "arbitrary"),
        name="matmul_w32",
    )(x, w_stack)


def _glu_kernel(x_ref, wg_ref, wu_ref, o_ref, *, nj_valid):
    j = pl.program_id(1)

    @pl.when(j < nj_valid)
    def _():
        x = x_ref[...]
        g = jnp.dot(x, wg_ref[...].astype(BF16), preferred_element_type=F32)
        u = jnp.dot(x, wu_ref[...].astype(BF16), preferred_element_type=F32)
        o_ref[...] = (_silu(g) * u).astype(BF16)

    @pl.when(j >= nj_valid)
    def _():
        o_ref[...] = jnp.zeros_like(o_ref)


def _glu(x, w_in, layer, i, f, fp):
    m, kd = x.shape
    bm = _row_block(m)
    bn = _largest_divisor(f, 256, LANE)
    assert fp % bn == 0
    nj_valid = f // bn
    last = nj_valid - 1
    return pl.pallas_call(
        functools.partial(_glu_kernel, nj_valid=nj_valid),
        grid=(m // bm, fp // bn),
        in_specs=[
            pl.BlockSpec((bm, kd), lambda r, j: (r, 0), pipeline_mode=pl.Buffered(1)),
            pl.BlockSpec((None, None, kd, bn), lambda r, j: (layer, i, 0, jnp.minimum(j, last))),
            pl.BlockSpec((None, None, kd, bn), lambda r, j: (layer, i, 0, nj_valid + jnp.minimum(j, last))),
        ],
        out_specs=pl.BlockSpec((bm, bn), lambda r, j: (r, j)),
        out_shape=jax.ShapeDtypeStruct((m, fp), BF16),
        compiler_params=_params("arbitrary", "arbitrary"),
        name="ffn_in_glu",
    )(x, w_in, w_in)


def _cast_kernel(w_ref, o_ref, *, n_valid):
    r = pl.program_id(0)

    @pl.when(r < n_valid)
    def _():
        o_ref[...] = w_ref[...].astype(BF16)

    @pl.when(r >= n_valid)
    def _():
        o_ref[...] = jnp.zeros_like(o_ref)


def _cast_rows(w_stack, lead, rows_out):
    kd, n = w_stack.shape[-2:]
    br = _largest_divisor(kd, 256, BF16_SUBLANE)
    assert rows_out % br == 0
    n_valid = kd // br
    none = (None,) * len(lead)
    return pl.pallas_call(
        functools.partial(_cast_kernel, n_valid=n_valid),
        grid=(rows_out // br,),
        in_specs=[pl.BlockSpec(none + (br, n), lambda r: lead + (jnp.minimum(r, n_valid - 1), 0))],
        out_specs=pl.BlockSpec((br, n), lambda r: (r, 0)),
        out_shape=jax.ShapeDtypeStruct((rows_out, n), BF16),
        compiler_params=_params("arbitrary"),
        name="cast_weight",
    )(w_stack)


def _gelu_ln(v, ln_g, ln_b):
    gv = jax.nn.gelu(v)
    mu = jnp.mean(gv, axis=-1, keepdims=True)
    cen = gv - mu
    var = jnp.mean(cen * cen, axis=-1, keepdims=True)
    return cen * lax.rsqrt(var + EPS) * ln_g + ln_b


def _gmlp_p_kernel(u_ref, v_ref, lng_ref, lnb_ref, ws_ref, bs_ref, o_ref, *, n_heads):
    c = GM_CHUNK
    vn = _gelu_ln(v_ref[...], lng_ref[...], lnb_ref[...]).astype(BF16)
    tril = lax.broadcasted_iota(jnp.int32, (c, c), 0) >= lax.broadcasted_iota(jnp.int32, (c, c), 1)
    for h in range(n_heads):
        cols = slice(h * LANE, (h + 1) * LANE)
        w = jnp.where(tril, ws_ref[h], 0.0).astype(BF16)
        f = jnp.dot(w, vn[:, cols], preferred_element_type=F32) + bs_ref[:, cols]
        o_ref[:, cols] = (jax.nn.gelu(u_ref[:, cols]) * f).astype(BF16)


def _gmlp_prompt(p, tp, d, ln_g, ln_b, ws, bs_exp):
    n_heads = d // LANE
    c = GM_CHUNK
    m = p.shape[0]
    return pl.pallas_call(
        functools.partial(_gmlp_p_kernel, n_heads=n_heads),
        grid=(tp // c,),
        in_specs=[
            pl.BlockSpec((c, d), lambda i: (i, 0)),
            pl.BlockSpec((c, d), lambda i: (i, 1)),
            pl.BlockSpec((1, d), lambda i: (0, 0)),
            pl.BlockSpec((1, d), lambda i: (0, 0)),
            pl.BlockSpec((n_heads, c, c), lambda i: (0, 0, 0)),
            pl.BlockSpec((c, d), lambda i: (0, 0)),
        ],
        out_specs=pl.BlockSpec((c, d), lambda i: (i, 0)),
        out_shape=jax.ShapeDtypeStruct((m, 2 * d), BF16),
        compiler_params=_params("arbitrary"),
        name="gmlp_prompt",
    )(p, p, ln_g, ln_b, ws, bs_exp)


def _gmlp_s_kernel(u_ref, v_ref, lng_ref, lnb_ref, wsx_ref, bsx_ref, ya_ref, vn_ref, vnb, facc, *, t_len):
    t = pl.program_id(0)
    vn = _gelu_ln(v_ref[...], lng_ref[...], lnb_ref[...])
    vn_ref[...] = vn
    vnb[t] = vn.astype(BF16)
    facc[...] = jnp.broadcast_to(bsx_ref[t], facc.shape)
    w_t = wsx_ref[t]
    for s in range(t_len):
        @pl.when(s <= t)
        def _():
            w = w_t[s:s + 1, :].astype(BF16).astype(F32)
            facc[...] += w * vnb[s].astype(F32)
    ya_ref[...] = (jax.nn.gelu(u_ref[...]) * facc[...]).astype(BF16)


def _gmlp_sample(uvz3, d, ln_g, ln_b, wsx, bsx):
    t_len, bs, _ = uvz3.shape
    return pl.pallas_call(
        functools.partial(_gmlp_s_kernel, t_len=t_len),
        grid=(t_len,),
        in_specs=[
            pl.BlockSpec((None, bs, d), lambda t: (t, 0, 0)),
            pl.BlockSpec((None, bs, d), lambda t: (t, 0, 1)),
            pl.BlockSpec((1, d), lambda t: (0, 0)),
            pl.BlockSpec((1, d), lambda t: (0, 0)),
            pl.BlockSpec((t_len, t_len, d), lambda t: (0, 0, 0)),
            pl.BlockSpec((t_len, 1, d), lambda t: (0, 0, 0)),
        ],
        out_specs=[
            pl.BlockSpec((None, bs, d), lambda t: (t, 0, 0)),
            pl.BlockSpec((None, bs, d), lambda t: (t, 0, 0)),
        ],
        out_shape=[
            jax.ShapeDtypeStruct((t_len, bs, d), BF16),
            jax.ShapeDtypeStruct((t_len, bs, d), F32),
        ],
        scratch_shapes=[pltpu.VMEM((t_len, bs, d), BF16), pltpu.VMEM((bs, d), F32)],
        compiler_params=_params("arbitrary"),
        name="gmlp_sample",
    )(uvz3, uvz3, ln_g, ln_b, wsx, bsx)


def _head_expand(gw):
    row = lax.broadcasted_iota(jnp.int32, (LANE, gw), 0)
    col = lax.broadcasted_iota(jnp.int32, (LANE, gw), 1)
    return (lax.shift_right_logical(col, 6) == row).astype(F32)


def _gated_group_norm(y, z, ng):
    y = y * _silu(z)
    return (_rms(y) * ng).astype(BF16)


def _ssd_p_kernel(x_ref, b_ref, c_ref, z_ref, dt_ref, cwx_ref, cwb_ref, cwc_ref, cbx_ref, cbb_ref, cbc_ref,
                  dtb_ref, alog_ref, dskip_ref, ng_ref, ymix_ref,
                  y_ref, cx_out, cb_out, cc_out, h_out,
                  xpad, bpad, cpad, state, ybuf, *, hpg):
    del ymix_ref
    n = SSD_CHUNK
    tail = CONV_W - 1
    first = SUBLANE - tail
    ci = pl.program_id(2)

    @pl.when(ci == 0)
    def _():
        state[...] = jnp.zeros_like(state)
        for pad in (xpad, bpad, cpad):
            pad[pl.ds(0, SUBLANE), :] = jnp.zeros((SUBLANE, pad.shape[1]), F32)

    def conv_act(src_ref, pad_ref, cw_ref, cb_ref, out_ref):
        x = src_ref[...]
        pad_ref[pl.ds(SUBLANE, n), :] = x
        acc = cb_ref[...] + cw_ref[0:1, :] * pad_ref[pl.ds(first, n), :]
        for k in range(1, tail):
            acc = acc + cw_ref[k:k + 1, :] * pad_ref[pl.ds(first + k, n), :]
        acc = acc + cw_ref[tail:tail + 1, :] * x
        last = src_ref[pl.ds(n - tail, tail), :]
        pad_ref[pl.ds(first, tail), :] = last
        out_ref[...] = last
        return _silu(acc)

    xs = conv_act(x_ref, xpad, cwx_ref, cbx_ref, cx_out)
    bm = conv_act(b_ref, bpad, cwb_ref, cbb_ref, cb_out).astype(BF16)
    cm = conv_act(c_ref, cpad, cwc_ref, cbc_ref, cc_out).astype(BF16)

    gw = xs.shape[1]
    dt = jax.nn.softplus(dt_ref[...] + dtb_ref[...])
    a = dt * (-jnp.exp(alog_ref[...]))
    tril = lax.broadcasted_iota(jnp.int32, (n, n), 0) >= lax.broadcasted_iota(jnp.int32, (n, n), 1)
    a_cum = jnp.dot(tril.astype(F32), a, precision=HIGHEST, preferred_element_type=F32)
    a_cum_t = a_cum.T
    expand = _head_expand(gw)
    dt_e = jnp.dot(dt, expand, precision=HIGHEST, preferred_element_type=F32)
    acum_e = jnp.dot(a_cum, expand, precision=HIGHEST, preferred_element_type=F32)
    dtx = dt_e * xs
    w_end = (jnp.exp(acum_e[n - 1:n, :] - acum_e) * dtx).astype(BF16)
    dtx_b = dtx.astype(BF16)
    exp_acum = jnp.exp(acum_e)
    cb = lax.dot_general(cm, bm, NT_DIMS, preferred_element_type=F32)
    a_last = a_cum[n - 1:n, :]
    for j in range(hpg):
        cols = slice(j * HEAD_P, (j + 1) * HEAD_P)
        seg = a_cum[:, j:j + 1] - a_cum_t[j:j + 1, :]
        decay = jnp.exp(jnp.where(tril, seg, -jnp.inf))
        y_intra = jnp.dot((cb * decay).astype(BF16), dtx_b[:, cols], preferred_element_type=F32)
        h_in = state[j]
        y_inter = lax.dot_general(cm, h_in.astype(BF16), NT_DIMS, preferred_element_type=F32)
        ybuf[:, cols] = y_intra + y_inter * exp_acum[:, cols]
        new = lax.dot_general(w_end[:, cols], bm, TN_DIMS, preferred_element_type=F32)
        state[j] = h_in * jnp.exp(a_last[:, j:j + 1]) + new
    y = ybuf[...] + dskip_ref[...] * xs
    y_ref[...] = _gated_group_norm(y, z_ref[...], ng_ref[...])
    h_out[...] = state[...]


def _ssd_prompt(p, dtp, y_mix, bp, seq, d, cw8, cb1, dtb_g, alog_g, dskip_e, ng):
    g = N_GROUPS
    gw = d // g
    hpg = gw // HEAD_P
    n = SSD_CHUNK
    nc = seq // n
    tail = CONV_W - 1
    b_off = d // D_STATE
    c_off = b_off + g
    z_off = 2 * d // gw
    x_off = 3 * d // gw
    pb_off = 4 * d // D_STATE
    pc_off = pb_off + g

    def rows(width, off):
        return pl.BlockSpec((n, width), lambda b, gi, c: (b * nc + c, off + gi))

    def par(r, width, off):
        return pl.BlockSpec((r, width), lambda b, gi, c: (0, off + gi))

    def grp():
        return pl.BlockSpec((None, 1, LANE), lambda b, gi, c: (gi, 0, 0))

    def conv_out(width):
        return pl.BlockSpec((None, tail, width), lambda b, gi, c: (b, 0, gi))

    outs = pl.pallas_call(
        functools.partial(_ssd_p_kernel, hpg=hpg),
        grid=(bp, g, nc),
        in_specs=[
            rows(gw, x_off), rows(D_STATE, pb_off), rows(D_STATE, pc_off), rows(gw, z_off), rows(LANE, 0),
            par(SUBLANE, gw, 0), par(SUBLANE, D_STATE, b_off), par(SUBLANE, D_STATE, c_off),
            par(1, gw, 0), par(1, D_STATE, b_off), par(1, D_STATE, c_off),
            grp(), grp(), par(1, gw, 0), par(1, gw, 0),
            pl.BlockSpec(memory_space=pl.ANY),
        ],
        out_specs=[
            pl.BlockSpec((n, gw), lambda b, gi, c: (b * nc + c, g + gi)),
            conv_out(gw), conv_out(D_STATE), conv_out(D_STATE),
            pl.BlockSpec((None, hpg, HEAD_P, D_STATE), lambda b, gi, c: (b, gi, 0, 0)),
        ],
        input_output_aliases={15: 0},
        out_shape=[
            jax.ShapeDtypeStruct(y_mix.shape, BF16),
            jax.ShapeDtypeStruct((bp, tail, d), F32),
            jax.ShapeDtypeStruct((bp, tail, g * D_STATE), F32),
            jax.ShapeDtypeStruct((bp, tail, g * D_STATE), F32),
            jax.ShapeDtypeStruct((bp, g * hpg, HEAD_P, D_STATE), F32),
        ],
        scratch_shapes=[
            pltpu.VMEM((SUBLANE + n, gw), F32),
            pltpu.VMEM((SUBLANE + n, D_STATE), F32),
            pltpu.VMEM((SUBLANE + n, D_STATE), F32),
            pltpu.VMEM((hpg, HEAD_P, D_STATE), F32),
            pltpu.VMEM((n, gw), F32),
        ],
        compiler_params=_params("arbitrary", "arbitrary", "arbitrary"),
        name="ssd_prompt",
    )(p, p, p, p, dtp, cw8, cw8, cw8, cb1, cb1, cb1, dtb_g, alog_g, dskip_e, ng, y_mix)
    y_mix, cx, cb_, cc, h_t = outs
    return y_mix, jnp.concatenate([cx, cb_, cc], axis=-1), h_t


def _ssd_s_kernel(x_ref, b_ref, c_ref, z_ref, dt_ref, sx_ref, sb_ref, sc_ref,
                  cwx_ref, cwb_ref, cwc_ref, cbx_ref, cbb_ref, cbc_ref,
                  dtb_ref, alog_ref, dskip_ref, ng_ref, h0_ref, *rest, t_len, sb, chained):
    y_ref, cx_out, cb_out, cc_out, h_out, cbuf, bbuf, wbuf, yibuf = rest[1:] if chained else rest
    tail = CONV_W - 1

    def conv_act(src_ref, st_ref, cw_ref, cb_ref, out_ref):
        full = [st_ref[r] for r in range(tail)] + [src_ref[t] for t in range(t_len)]
        acts = []
        for t in range(t_len):
            acc = cb_ref[...] + cw_ref[0:1, :] * full[t]
            for k in range(1, CONV_W):
                acc = acc + cw_ref[k:k + 1, :] * full[t + k]
            acts.append(_silu(acc))
        for r in range(tail):
            out_ref[r] = full[t_len + r]
        return acts

    xs = conv_act(x_ref, sx_ref, cwx_ref, cbx_ref, cx_out)
    bm = conv_act(b_ref, sb_ref, cwb_ref, cbb_ref, cb_out)
    cm = conv_act(c_ref, sc_ref, cwc_ref, cbc_ref, cc_out)
    gw = xs[0].shape[1]

    a_neg = -jnp.exp(alog_ref[...])
    dts, acums = [], []
    run = None
    for t in range(t_len):
        dt = jax.nn.softplus(dt_ref[t] + dtb_ref[...])
        run = dt * a_neg if run is None else run + dt * a_neg
        dts.append(dt)
        acums.append(run)
    expand = _head_expand(gw)
    dt_e = jnp.dot(jnp.concatenate(dts, axis=0), expand, precision=HIGHEST, preferred_element_type=F32)
    acum_all = jnp.dot(jnp.concatenate(acums, axis=0), expand, precision=HIGHEST, preferred_element_type=F32)
    acum_e = [acum_all[t * sb:(t + 1) * sb] for t in range(t_len)]
    dtx = [dt_e[t * sb:(t + 1) * sb] * xs[t] for t in range(t_len)]
    a_last = acum_e[t_len - 1]

    n_panels = gw // LANE

    def put_rows(buf, row0, val):
        for p in range(n_panels):
            buf[p, pl.ds(row0, sb), :] = val[:, p * LANE:(p + 1) * LANE]

    def seq_rows(buf, q):
        return jnp.concatenate([buf[p, pl.ds(q, SUBLANE, stride=sb), :] for p in range(n_panels)], axis=1)

    zero_rows = (SUBLANE - t_len) * sb
    for buf in (cbuf, bbuf):
        buf[pl.ds(t_len * sb, zero_rows), :] = jnp.zeros((zero_rows, D_STATE), F32)
    wbuf[:, pl.ds(t_len * sb, zero_rows), :] = jnp.zeros((n_panels, zero_rows, LANE), F32)
    for t in range(t_len):
        cbuf[pl.ds(t * sb, sb), :] = cm[t]
        bbuf[pl.ds(t * sb, sb), :] = bm[t]
        put_rows(wbuf, t * sb, jnp.exp(a_last - acum_e[t]) * dtx[t])
    chunk_decay = jnp.exp(a_last)
    ones = jnp.ones((SUBLANE, D_STATE), BF16)
    for q in range(sb):
        c_q = cbuf[pl.ds(q, SUBLANE, stride=sb), :].astype(BF16)
        b_q = bbuf[pl.ds(q, SUBLANE, stride=sb), :].astype(BF16)
        w_q = seq_rows(wbuf, q).astype(BF16)
        h0 = h0_ref[q]
        y_q = lax.dot_general(c_q, h0.astype(BF16), NT_DIMS, preferred_element_type=F32)
        for p in range(n_panels):
            yibuf[p, pl.ds(q, SUBLANE, stride=sb), :] = y_q[:, p * LANE:(p + 1) * LANE]
        new = lax.dot_general(w_q, b_q, TN_DIMS, preferred_element_type=F32)
        cd = chunk_decay[q:q + 1, :]
        hi = cd.astype(BF16).astype(F32)
        cd8 = jnp.concatenate([hi, cd - hi, jnp.zeros((SUBLANE - 2, gw), F32)], axis=0).astype(BF16)
        cd_col = lax.dot_general(cd8, ones, TN_DIMS, preferred_element_type=F32)
        h_out[q] = h0 * cd_col + new

    for t in range(t_len):
        y = None
        for s in range(t + 1):
            cbts = jnp.sum(cm[t] * bm[s], axis=-1, keepdims=True)
            term = cbts * jnp.exp(acum_e[t] - acum_e[s]) * dtx[s]
            y = term if y is None else y + term
        y_inter = jnp.concatenate([yibuf[p, pl.ds(t * sb, sb), :] for p in range(n_panels)], axis=1)
        y = y + y_inter * jnp.exp(acum_e[t]) + dskip_ref[...] * xs[t]
        y_ref[t] = _gated_group_norm(y, z_ref[t], ng_ref[...])


def _ssd_sample(p3, dt3, conv3, state_ssm4, ssm_prev, layer, d, cw8, cb1, dtb_g, alog_g, dskip_e, ng):
    t_len, bs, _ = p3.shape
    depth = state_ssm4.shape[0]
    assert t_len <= SUBLANE
    g = N_GROUPS
    gw = d // g
    tail = CONV_W - 1
    sb = SUBLANE
    b_off = d // D_STATE
    c_off = b_off + g
    z_off = 2 * d // gw
    x_off = 3 * d // gw
    pb_off = 4 * d // D_STATE
    pc_off = pb_off + g
    chained = ssm_prev is not None

    def slab(r, width, off):
        return pl.BlockSpec((r, sb, width), lambda i, gi: (0, i, off + gi))

    def par(r, width, off):
        return pl.BlockSpec((r, width), lambda i, gi: (0, off + gi))

    def grp():
        return pl.BlockSpec((None, 1, LANE), lambda i, gi: (gi, 0, 0))

    state_spec = pl.BlockSpec((None, sb, gw, D_STATE), lambda i, gi: (layer, i, gi, 0))
    in_specs = [
        slab(t_len, gw, x_off), slab(t_len, D_STATE, pb_off), slab(t_len, D_STATE, pc_off),
        slab(t_len, gw, z_off), slab(t_len, LANE, 0),
        slab(tail, gw, 0), slab(tail, D_STATE, b_off), slab(tail, D_STATE, c_off),
        par(SUBLANE, gw, 0), par(SUBLANE, D_STATE, b_off), par(SUBLANE, D_STATE, c_off),
        par(1, gw, 0), par(1, D_STATE, b_off), par(1, D_STATE, c_off),
        grp(), grp(), par(1, gw, 0), par(1, gw, 0),
        state_spec,
    ]
    args = [p3, p3, p3, p3, dt3, conv3, conv3, conv3, cw8, cw8, cw8, cb1, cb1, cb1,
            dtb_g, alog_g, dskip_e, ng, state_ssm4]
    if chained:
        in_specs.append(pl.BlockSpec(memory_space=pl.ANY))
        args.append(ssm_prev)
    outs = pl.pallas_call(
        functools.partial(_ssd_s_kernel, t_len=t_len, sb=sb, chained=chained),
        grid=(bs // sb, g),
        in_specs=in_specs,
        out_specs=[
            slab(t_len, gw, 0), slab(tail, gw, 0), slab(tail, D_STATE, 0), slab(tail, D_STATE, 0),
            state_spec,
        ],
        input_output_aliases={len(args) - 1: 4} if chained else {},
        out_shape=[
            jax.ShapeDtypeStruct((t_len, bs, d), BF16),
            jax.ShapeDtypeStruct((tail, bs, d), F32),
            jax.ShapeDtypeStruct((tail, bs, g * D_STATE), F32),
            jax.ShapeDtypeStruct((tail, bs, g * D_STATE), F32),
            jax.ShapeDtypeStruct((depth, bs, d, D_STATE), F32),
        ],
        scratch_shapes=[
            pltpu.VMEM((SUBLANE * sb, D_STATE), F32),
            pltpu.VMEM((SUBLANE * sb, D_STATE), F32),
            pltpu.VMEM((gw // LANE, SUBLANE * sb, LANE), F32),
            pltpu.VMEM((gw // LANE, SUBLANE * sb, LANE), F32),
        ],
        compiler_params=_params("arbitrary", "arbitrary"),
        name="ssd_sample",
    )(*args)
    yb, cx, cb_, cc, h_t = outs
    return yb, jnp.concatenate([cx, cb_, cc], axis=-1), h_t


def _group_pad(v):
    hpg = v.shape[0] // N_GROUPS
    return jnp.pad(v.reshape(N_GROUPS, hpg), ((0, 0), (0, LANE - hpg))).reshape(N_GROUPS, 1, LANE)


def kernel(x_prompt, x_sample, state_conv, state_ssm, c_prompt, c_sample, w_ada, b_ada, norm_pre, norm_post,
           ffn_w_in, ffn_w_out, w_in_mix, w_out_mix, gm_ln_g, gm_ln_b, gm_ws, gm_bs, conv_w, conv_b,
           dt_bias, a_log, d_skip, ssm_norm_g):
    bp, seq, d = x_prompt.shape
    bs, t_len, _ = x_sample.shape
    depth = w_ada.shape[0]
    f = ffn_w_out.shape[2]
    h_b = dt_bias.shape[1]
    conv_dim = conv_w.shape[2]
    g = N_GROUPS
    hpg = h_b // g
    tail = CONV_W - 1
    rows = _Rows(bp, seq, bs, t_len, d)
    tp = rows.tp
    fp = -(-f // 1024) * 1024
    n_proj = 3 * d + conv_dim

    x = jnp.concatenate([x_prompt.reshape(tp, d), x_sample.transpose(1, 0, 2).reshape(rows.ts, d)], axis=0)
    mc = -(-(bs + bp) // BF16_SUBLANE) * BF16_SUBLANE
    c_all = jnp.concatenate([c_sample, c_prompt, jnp.zeros((mc - bs - bp, d), F32)], axis=0)
    mod = _ada(c_all, w_ada, b_ada)
    mod_p = mod[:, bs:bs + bp].reshape(depth * bp * 3 * N_SUB, 1, d)
    npre = norm_pre.reshape(depth * N_SUB, 1, d)
    npost = norm_post.reshape(depth * N_SUB, 1, d)
    state_ssm4 = state_ssm.reshape(depth, bs, h_b * HEAD_P, D_STATE)

    def ffn(h, layer, i):
        a = _glu(h, ffn_w_in, layer, i, f, fp)
        return _mm_acc(a, _cast_rows(ffn_w_out, (layer, i), fp))

    outs = {k: [] for k in ("p_conv", "p_ssm", "s_conv", "s_v")}
    ssm_s = None
    h = _pre(rows, x, npre, mod_p, mod, 0, 0)
    for layer in range(depth):
        x, h = _post(rows, x, ffn(h, layer, 0), npre, npost, mod_p, mod, layer, 0, FFN_RES, (layer, 1))

        p = _mm_w32(h, w_in_mix, layer, n_proj)
        w_dt = w_in_mix[layer, :, n_proj:].reshape(d, g, hpg)
        w_dtp = jnp.pad(w_dt, ((0, 0), (0, 0), (0, LANE - hpg))).reshape(1, d, g * LANE)
        dtp = _mm_w32(h, w_dtp, 0, g * LANE)
        p3 = p[tp:].reshape(t_len, bs, n_proj)
        dt3 = dtp[tp:].reshape(t_len, bs, g * LANE)

        ln_g = gm_ln_g[layer].reshape(1, d)
        ln_b = gm_ln_b[layer].reshape(1, d)
        bs_exp = jnp.repeat(gm_bs[layer].T, LANE, axis=1)
        y_mix = _gmlp_prompt(p, tp, d, ln_g, ln_b, gm_ws[layer], bs_exp)
        wsx = jnp.repeat(gm_ws[layer][:, :t_len, :t_len].transpose(1, 2, 0), LANE, axis=2)
        bsx = jnp.repeat(gm_bs[layer][:, :t_len].T, LANE, axis=1).reshape(t_len, 1, d)
        ya_s, vn_s = _gmlp_sample(p3, d, ln_g, ln_b, wsx, bsx)

        cw8 = jnp.pad(conv_w[layer], ((0, SUBLANE - CONV_W), (0, 0)))
        cb1 = conv_b[layer].reshape(1, conv_dim)
        dtb_g, alog_g = _group_pad(dt_bias[layer]), _group_pad(a_log[layer])
        dskip_e = jnp.repeat(d_skip[layer], HEAD_P).reshape(1, d)
        ng = ssm_norm_g[layer].reshape(1, d)
        y_mix, conv_p, ssm_p = _ssd_prompt(p, dtp, y_mix, bp, seq, d, cw8, cb1, dtb_g, alog_g, dskip_e, ng)
        conv3 = state_conv[layer].transpose(1, 0, 2)
        yb_s, conv_s, ssm_s = _ssd_sample(p3, dt3, conv3, state_ssm4, ssm_s, layer, d,
                                          cw8, cb1, dtb_g, alog_g, dskip_e, ng)
        y_s = jnp.concatenate([ya_s.reshape(rows.ts, d), yb_s.reshape(rows.ts, d)], axis=1)
        y_mix = lax.dynamic_update_slice(y_mix, y_s, (tp, 0))

        o = _mm_acc(y_mix, _cast_rows(w_out_mix, (layer,), 2 * d))
        x, h = _post(rows, x, o, npre, npost, mod_p, mod, layer, 1, 1.0, (layer, 2))

        nxt = (layer + 1, 0) if layer + 1 < depth else None
        x, h = _post(rows, x, ffn(h, layer, 1), npre, npost, mod_p, mod, layer, 2, FFN_RES, nxt)

        outs["p_conv"].append(conv_p)
        outs["p_ssm"].append(ssm_p)
        outs["s_conv"].append(conv_s.transpose(1, 0, 2))
        outs["s_v"].append(vn_s.transpose(1, 0, 2))

    y_prompt = x[:tp].reshape(bp, seq, d)
    y_sample = x[tp:].reshape(t_len, bs, d).transpose(1, 0, 2)
    sample_ssm = ssm_s.reshape(depth, bs, h_b, HEAD_P, D_STATE)
    return (y_prompt, y_sample, jnp.stack(outs["p_conv"]), jnp.stack(outs["p_ssm"]),
            jnp.stack(outs["s_conv"]), sample_ssm, jnp.stack(outs["s_v"]))
```

```python
import functools

import jax
import jax.numpy as jnp
from jax import lax
from jax.experimental import pallas as pl
from jax.experimental.pallas import tpu as pltpu

F32 = jnp.float32
BF16 = jnp.bfloat16
HIGHEST = lax.Precision.HIGHEST

GM_CHUNK = 128
SSD_CHUNK = 128
HEAD_P = 64
D_STATE = 128
N_GROUPS = 8
CONV_W = 4
N_SUB = 3
FFN_RES = 0.5
EPS = 1e-6

LANE = 128
SUBLANE = 8
BF16_SUBLANE = 16
VMEM_LIMIT_BYTES = 56 * 2**20

NT_DIMS = (((1,), (1,)), ((), ()))
TN_DIMS = (((0,), (0,)), ((), ()))


def _params(*sem):
    return pltpu.CompilerParams(dimension_semantics=sem, vmem_limit_bytes=VMEM_LIMIT_BYTES)


def _largest_divisor(n, cap, mult):
    for d in range(min(cap, n), 0, -1):
        if n % d == 0 and d % mult == 0:
            return d
    raise ValueError(f"no tile for {n} (cap {cap}, multiple of {mult})")


def _silu(x):
    return x * jax.nn.sigmoid(x)


def _rms(x):
    return x * lax.rsqrt(jnp.mean(x * x, axis=-1, keepdims=True) + EPS)


def _ada_kernel(c_ref, w_ref, b_ref, o_ref):
    a = _silu(c_ref[...]).astype(BF16)
    o_ref[...] = jnp.dot(a, w_ref[...].astype(BF16), preferred_element_type=F32) + b_ref[...]


def _ada(c_all, w_ada, b_ada):
    depth, d, n = w_ada.shape
    mc = c_all.shape[0]
    bn = _largest_divisor(n, 512, LANE)
    return pl.pallas_call(
        _ada_kernel,
        grid=(depth, n // bn),
        in_specs=[
            pl.BlockSpec((mc, d), lambda l, j: (0, 0)),
            pl.BlockSpec((None, d, bn), lambda l, j: (l, 0, j)),
            pl.BlockSpec((None, 1, bn), lambda l, j: (l, 0, j)),
        ],
        out_specs=pl.BlockSpec((None, mc, bn), lambda l, j: (l, 0, j)),
        out_shape=jax.ShapeDtypeStruct((depth, mc, n), F32),
        compiler_params=_params("arbitrary", "arbitrary"),
        name="ada_mod",
    )(c_all, w_ada, b_ada.reshape(depth, 1, n))


def _norm_mod(x, g, scale, shift):
    return _rms(x) * g * (1.0 + scale) + shift


def _pre_kernel(x_ref, g_ref, scp_ref, shp_ref, scs_ref, shs_ref, h_ref, *, n_ptiles, bs):
    i = pl.program_id(0)

    @pl.when(i < n_ptiles)
    def _():
        h_ref[...] = _norm_mod(x_ref[...], g_ref[...], scp_ref[...], shp_ref[...]).astype(BF16)

    @pl.when(i >= n_ptiles)
    def _():
        for r in range(x_ref.shape[0] // bs):
            rows = pl.ds(r * bs, bs)
            h_ref[rows, :] = _norm_mod(x_ref[rows, :], g_ref[...], scs_ref[...], shs_ref[...]).astype(BF16)


def _post_kernel(x_ref, o_ref, gpost_ref, gtp_ref, gts_ref, *rest, res_w, n_ptiles, bs, with_next):
    if with_next:
        gpre_ref, scp_ref, shp_ref, scs_ref, shs_ref, xo_ref, h_ref = rest
    else:
        (xo_ref,) = rest
    i = pl.program_id(0)

    def update(rows, gate, scale, shift):
        xn = x_ref[rows, :] + res_w * gate * (_rms(o_ref[rows, :]) * gpost_ref[...])
        xo_ref[rows, :] = xn
        if with_next:
            h_ref[rows, :] = _norm_mod(xn, gpre_ref[...], scale, shift).astype(BF16)

    @pl.when(i < n_ptiles)
    def _():
        if with_next:
            update(slice(None), gtp_ref[...], scp_ref[...], shp_ref[...])
        else:
            update(slice(None), gtp_ref[...], None, None)

    @pl.when(i >= n_ptiles)
    def _():
        for r in range(x_ref.shape[0] // bs):
            rows = pl.ds(r * bs, bs)
            if with_next:
                update(rows, gts_ref[...], scs_ref[...], shs_ref[...])
            else:
                update(rows, gts_ref[...], None, None)


class _Rows:
    def __init__(self, bp, seq, bs, t, d):
        self.bp, self.seq, self.bs, self.t, self.d = bp, seq, bs, t, d
        self.tp, self.ts = bp * seq, bs * t
        self.m = self.tp + self.ts
        ks = [k for k in range(t, 0, -1) if t % k == 0 and seq % (k * bs) == 0 and (k * bs <= 256 or k == 1)]
        if not ks:
            raise ValueError("prompt length must be a multiple of the sample batch")
        self.bm = ks[0] * bs
        self.n_ptiles = self.tp // self.bm
        self.n_tiles = self.m // self.bm
        self.tiles_per_seq = seq // self.bm

    def row_spec(self):
        return pl.BlockSpec((self.bm, self.d), lambda i: (i, 0))

    def vec_spec(self, idx):
        return pl.BlockSpec((None, 1, self.d), lambda i: (idx, 0, 0))

    def modp_spec(self, layer, k):
        bp, tps = self.bp, self.tiles_per_seq
        return pl.BlockSpec(
            (None, 1, self.d),
            lambda i: ((layer * bp + jnp.minimum(i // tps, bp - 1)) * (3 * N_SUB) + k, 0, 0))

    def mods_spec(self, layer, k):
        return pl.BlockSpec((None, self.bs, self.d), lambda i: (layer, 0, k))


def _pre(rows, x, norm_pre, mod_p, mod, layer, sub):
    kern = functools.partial(_pre_kernel, n_ptiles=rows.n_ptiles, bs=rows.bs)
    k_shift, k_scale = sub * 3, sub * 3 + 1
    return pl.pallas_call(
        kern,
        grid=(rows.n_tiles,),
        in_specs=[
            rows.row_spec(),
            rows.vec_spec(layer * N_SUB + sub),
            rows.modp_spec(layer, k_scale), rows.modp_spec(layer, k_shift),
            rows.mods_spec(layer, k_scale), rows.mods_spec(layer, k_shift),
        ],
        out_specs=rows.row_spec(),
        out_shape=jax.ShapeDtypeStruct((rows.m, rows.d), BF16),
        compiler_params=_params("arbitrary"),
        name="pre_norm",
    )(x, norm_pre, mod_p, mod_p, mod, mod)


def _post(rows, x, o, norm_pre, norm_post, mod_p, mod, layer, sub, res_w, nxt):
    with_next = nxt is not None
    kern = functools.partial(_post_kernel, res_w=res_w, n_ptiles=rows.n_ptiles, bs=rows.bs, with_next=with_next)
    k_gate = sub * 3 + 2
    in_specs = [
        rows.row_spec(), rows.row_spec(),
        rows.vec_spec(layer * N_SUB + sub),
        rows.modp_spec(layer, k_gate), rows.mods_spec(layer, k_gate),
    ]
    args = [x, o, norm_post, mod_p, mod]
    out_specs = [rows.row_spec()]
    out_shape = [jax.ShapeDtypeStruct((rows.m, rows.d), F32)]
    if with_next:
        nl, ns = nxt
        in_specs += [
            rows.vec_spec(nl * N_SUB + ns),
            rows.modp_spec(nl, ns * 3 + 1), rows.modp_spec(nl, ns * 3),
            rows.mods_spec(nl, ns * 3 + 1), rows.mods_spec(nl, ns * 3),
        ]
        args += [norm_pre, mod_p, mod_p, mod, mod]
        out_specs.append(rows.row_spec())
        out_shape.append(jax.ShapeDtypeStruct((rows.m, rows.d), BF16))
    res = pl.pallas_call(
        kern,
        grid=(rows.n_tiles,),
        in_specs=in_specs,
        out_specs=out_specs,
        out_shape=out_shape,
        compiler_params=_params("arbitrary"),
        name="post_residual",
    )(*args)
    return (res[0], res[1]) if with_next else (res[0], None)


def _mm_acc_kernel(x_ref, w_ref, o_ref):
    k = pl.program_id(2)

    @pl.when(k == 0)
    def _():
        o_ref[...] = jnp.dot(x_ref[...], w_ref[...], preferred_element_type=F32)

    @pl.when(k > 0)
    def _():
        o_ref[...] += jnp.dot(x_ref[...], w_ref[...], preferred_element_type=F32)


def _mm_acc(x, w):
    m, kd = x.shape
    n = w.shape[1]
    bm = _largest_divisor(m, 1088, BF16_SUBLANE)
    bn = _largest_divisor(n, 2048, LANE)
    bk = _largest_divisor(kd, 1024, LANE)
    return pl.pallas_call(
        _mm_acc_kernel,
        grid=(m // bm, n // bn, kd // bk),
        in_specs=[
            pl.BlockSpec((bm, bk), lambda i, j, k: (i, k)),
            pl.BlockSpec((bk, bn), lambda i, j, k: (k, j)),
        ],
        out_specs=pl.BlockSpec((bm, bn), lambda i, j, k: (i, j)),
        out_shape=jax.ShapeDtypeStruct((m, n), F32),
        compiler_params=_params("arbitrary", "arbitrary", "arbitrary"),
        name="matmul_acc",
    )(x, w)


def _mm_w32_kernel(x_ref, w_ref, o_ref):
    o_ref[...] = jnp.dot(x_ref[...], w_ref[...].astype(BF16), preferred_element_type=F32)


def _row_block(m):
    return _largest_divisor(m, 2176, BF16_SUBLANE)


def _mm_w32(x, w_stack, layer, n_cols, *, bn=512):
    m, kd = x.shape
    bm = _row_block(m)
    bn = _largest_divisor(n_cols, bn, LANE)
    return pl.pallas_call(
        _mm_w32_kernel,
        grid=(m // bm, n_cols // bn),
        in_specs=[
            pl.BlockSpec((bm, kd), lambda i, j: (i, 0), pipeline_mode=pl.Buffered(1)),
            pl.BlockSpec((None, kd, bn), lambda i, j: (layer, 0, j)),
        ],
        out_specs=pl.BlockSpec((bm, bn), lambda i, j: (i, j)),
        out_shape=jax.ShapeDtypeStruct((m, n_cols), F32),
        compiler_params=_params("arbitrary", "arbitrary"),
        name="matmul_w32",
    )(x, w_stack)


def _glu_kernel(x_ref, wg_ref, wu_ref, o_ref, *, nj_valid):
    j = pl.program_id(1)

    @pl.when(j < nj_valid)
    def _():
        x = x_ref[...]
        g = jnp.dot(x, wg_ref[...].astype(BF16), preferred_element_type=F32)
        u = jnp.dot(x, wu_ref[...].astype(BF16), preferred_element_type=F32)
        o_ref[...] = (_silu(g) * u).astype(BF16)

    @pl.when(j >= nj_valid)
    def _():
        o_ref[...] = jnp.zeros_like(o_ref)


def _glu(x, w_in, layer, i, f, fp):
    m, kd = x.shape
    bm = _row_block(m)
    bn = _largest_divisor(f, 256, LANE)
    assert fp % bn == 0
    nj_valid = f // bn
    last = nj_valid - 1
    return pl.pallas_call(
        functools.partial(_glu_kernel, nj_valid=nj_valid),
        grid=(m // bm, fp // bn),
        in_specs=[
            pl.BlockSpec((bm, kd), lambda r, j: (r, 0), pipeline_mode=pl.Buffered(1)),
            pl.BlockSpec((None, None, kd, bn), lambda r, j: (layer, i, 0, jnp.minimum(j, last))),
            pl.BlockSpec((None, None, kd, bn), lambda r, j: (layer, i, 0, nj_valid + jnp.minimum(j, last))),
        ],
        out_specs=pl.BlockSpec((bm, bn), lambda r, j: (r, j)),
        out_shape=jax.ShapeDtypeStruct((m, fp), BF16),
        compiler_params=_params("arbitrary", "arbitrary"),
        name="ffn_in_glu",
    )(x, w_in, w_in)


def _tail_cols_kernel(w_ref, o_ref, *, n_valid):
    lane = lax.broadcasted_iota(jnp.int32, o_ref.shape, 1)
    o_ref[...] = jnp.where(lane < n_valid, w_ref[...], 0.0)


def _tail_cols(w_stack, layer, col0):
    _, kd, n_all = w_stack.shape
    n_valid = n_all - col0
    assert col0 % LANE == 0 and 0 < n_valid <= LANE
    br = _largest_divisor(kd, 1024, SUBLANE)
    return pl.pallas_call(
        functools.partial(_tail_cols_kernel, n_valid=n_valid),
        grid=(kd // br,),
        in_specs=[pl.BlockSpec((None, br, LANE), lambda r: (layer, r, col0 // LANE))],
        out_specs=pl.BlockSpec((br, LANE), lambda r: (r, 0)),
        out_shape=jax.ShapeDtypeStruct((kd, LANE), F32),
        compiler_params=_params("arbitrary"),
        name="tail_cols",
    )(w_stack)


def _cast_kernel(w_ref, o_ref, *, n_valid):
    r = pl.program_id(0)

    @pl.when(r < n_valid)
    def _():
        o_ref[...] = w_ref[...].astype(BF16)

    @pl.when(r >= n_valid)
    def _():
        o_ref[...] = jnp.zeros_like(o_ref)


def _cast_rows(w_stack, lead, rows_out):
    kd, n = w_stack.shape[-2:]
    br = _largest_divisor(kd, 256, BF16_SUBLANE)
    assert rows_out % br == 0
    n_valid = kd // br
    none = (None,) * len(lead)
    return pl.pallas_call(
        functools.partial(_cast_kernel, n_valid=n_valid),
        grid=(rows_out // br,),
        in_specs=[pl.BlockSpec(none + (br, n), lambda r: lead + (jnp.minimum(r, n_valid - 1), 0))],
        out_specs=pl.BlockSpec((br, n), lambda r: (r, 0)),
        out_shape=jax.ShapeDtypeStruct((rows_out, n), BF16),
        compiler_params=_params("arbitrary"),
        name="cast_weight",
    )(w_stack)


def _gelu_ln(v, ln_g, ln_b):
    gv = jax.nn.gelu(v)
    mu = jnp.mean(gv, axis=-1, keepdims=True)
    cen = gv - mu
    var = jnp.mean(cen * cen, axis=-1, keepdims=True)
    return cen * lax.rsqrt(var + EPS) * ln_g + ln_b


def _gmlp_p_kernel(u_ref, v_ref, lng_ref, lnb_ref, ws_ref, bs_ref, o_ref, *, n_heads):
    c = GM_CHUNK
    vn = _gelu_ln(v_ref[...], lng_ref[...], lnb_ref[...]).astype(BF16)
    tril = lax.broadcasted_iota(jnp.int32, (c, c), 0) >= lax.broadcasted_iota(jnp.int32, (c, c), 1)
    for h in range(n_heads):
        cols = slice(h * LANE, (h + 1) * LANE)
        w = jnp.where(tril, ws_ref[h], 0.0).astype(BF16)
        f = jnp.dot(w, vn[:, cols], preferred_element_type=F32) + bs_ref[:, cols]
        o_ref[:, cols] = (jax.nn.gelu(u_ref[:, cols]) * f).astype(BF16)


def _gmlp_prompt(p, tp, d, ln_g, ln_b, ws, bs_exp):
    n_heads = d // LANE
    c = GM_CHUNK
    m = p.shape[0]
    return pl.pallas_call(
        functools.partial(_gmlp_p_kernel, n_heads=n_heads),
        grid=(tp // c,),
        in_specs=[
            pl.BlockSpec((c, d), lambda i: (i, 0)),
            pl.BlockSpec((c, d), lambda i: (i, 1)),
            pl.BlockSpec((1, d), lambda i: (0, 0)),
            pl.BlockSpec((1, d), lambda i: (0, 0)),
            pl.BlockSpec((n_heads, c, c), lambda i: (0, 0, 0)),
            pl.BlockSpec((c, d), lambda i: (0, 0)),
        ],
        out_specs=pl.BlockSpec((c, d), lambda i: (i, 0)),
        out_shape=jax.ShapeDtypeStruct((m, 2 * d), BF16),
        compiler_params=_params("arbitrary"),
        name="gmlp_prompt",
    )(p, p, ln_g, ln_b, ws, bs_exp)


def _gmlp_s_kernel(u_ref, v_ref, lng_ref, lnb_ref, wsx_ref, bsx_ref, ya_ref, vn_ref, vnb, facc, *, t_len):
    t = pl.program_id(0)
    vn = _gelu_ln(v_ref[...], lng_ref[...], lnb_ref[...])
    vn_ref[...] = vn
    vnb[t] = vn.astype(BF16)
    facc[...] = jnp.broadcast_to(bsx_ref[t], facc.shape)
    w_t = wsx_ref[t]
    for s in range(t_len):
        @pl.when(s <= t)
        def _():
            w = w_t[s:s + 1, :].astype(BF16).astype(F32)
            facc[...] += w * vnb[s].astype(F32)
    ya_ref[...] = (jax.nn.gelu(u_ref[...]) * facc[...]).astype(BF16)


def _gmlp_sample(uvz3, d, ln_g, ln_b, wsx, bsx):
    t_len, bs, _ = uvz3.shape
    return pl.pallas_call(
        functools.partial(_gmlp_s_kernel, t_len=t_len),
        grid=(t_len,),
        in_specs=[
            pl.BlockSpec((None, bs, d), lambda t: (t, 0, 0)),
            pl.BlockSpec((None, bs, d), lambda t: (t, 0, 1)),
            pl.BlockSpec((1, d), lambda t: (0, 0)),
            pl.BlockSpec((1, d), lambda t: (0, 0)),
            pl.BlockSpec((t_len, t_len, d), lambda t: (0, 0, 0)),
            pl.BlockSpec((t_len, 1, d), lambda t: (0, 0, 0)),
        ],
        out_specs=[
            pl.BlockSpec((None, bs, d), lambda t: (t, 0, 0)),
            pl.BlockSpec((None, bs, d), lambda t: (t, 0, 0)),
        ],
        out_shape=[
            jax.ShapeDtypeStruct((t_len, bs, d), BF16),
            jax.ShapeDtypeStruct((t_len, bs, d), F32),
        ],
        scratch_shapes=[pltpu.VMEM((t_len, bs, d), BF16), pltpu.VMEM((bs, d), F32)],
        compiler_params=_params("arbitrary"),
        name="gmlp_sample",
    )(uvz3, uvz3, ln_g, ln_b, wsx, bsx)


def _head_expand(gw):
    row = lax.broadcasted_iota(jnp.int32, (LANE, gw), 0)
    col = lax.broadcasted_iota(jnp.int32, (LANE, gw), 1)
    return (lax.shift_right_logical(col, 6) == row).astype(BF16)


def _split3(x):
    x1 = x.astype(BF16)
    r1 = x - x1.astype(F32)
    x2 = r1.astype(BF16)
    x3 = (r1 - x2.astype(F32)).astype(BF16)
    return x1, x2, x3


def _dot_f32_by_01(x, m01):
    r = x.shape[0]
    y = jnp.dot(jnp.concatenate(_split3(x), axis=0), m01, preferred_element_type=F32)
    return y[:r] + y[r:2 * r] + y[2 * r:]


def _dot_01_by_f32(m01, x):
    n = x.shape[1]
    y = jnp.dot(m01, jnp.concatenate(_split3(x), axis=1), preferred_element_type=F32)
    return y[:, :n] + y[:, n:2 * n] + y[:, 2 * n:]


def _gated_group_norm(y, z, ng):
    y = y * _silu(z)
    return (_rms(y) * ng).astype(BF16)


def _ssd_p_kernel(x_ref, b_ref, c_ref, z_ref, dt_ref, cwx_ref, cwb_ref, cwc_ref, cbx_ref, cbb_ref, cbc_ref,
                  dtb_ref, alog_ref, dskip_ref, ng_ref, ymix_ref,
                  y_ref, cx_out, cb_out, cc_out, h_out,
                  xpad, bpad, cpad, state_t, ybuf, *, hpg, n_chunks):
    del ymix_ref
    n = SSD_CHUNK
    tail = CONV_W - 1
    first = SUBLANE - tail
    ci = pl.program_id(2)

    @pl.when(ci == 0)
    def _():
        state_t[...] = jnp.zeros_like(state_t)
        for pad in (xpad, bpad, cpad):
            pad[pl.ds(0, SUBLANE), :] = jnp.zeros((SUBLANE, pad.shape[1]), F32)

    def conv_act(src_ref, pad_ref, cw_ref, cb_ref, out_ref):
        x = src_ref[...]
        pad_ref[pl.ds(SUBLANE, n), :] = x
        acc = cb_ref[...] + cw_ref[0:1, :] * pad_ref[pl.ds(first, n), :]
        for k in range(1, tail):
            acc = acc + cw_ref[k:k + 1, :] * pad_ref[pl.ds(first + k, n), :]
        acc = acc + cw_ref[tail:tail + 1, :] * x
        last = src_ref[pl.ds(n - tail, tail), :]
        pad_ref[pl.ds(first, tail), :] = last
        out_ref[...] = last
        return _silu(acc)

    xs = conv_act(x_ref, xpad, cwx_ref, cbx_ref, cx_out)
    bm = conv_act(b_ref, bpad, cwb_ref, cbb_ref, cb_out).astype(BF16)
    cm = conv_act(c_ref, cpad, cwc_ref, cbc_ref, cc_out).astype(BF16)

    gw = xs.shape[1]
    dt = jax.nn.softplus(dt_ref[...] + dtb_ref[...])
    a = dt * (-jnp.exp(alog_ref[...]))
    tril = lax.broadcasted_iota(jnp.int32, (n, n), 0) >= lax.broadcasted_iota(jnp.int32, (n, n), 1)
    a_cum = _dot_01_by_f32(tril.astype(BF16), a)
    a_cum_t = a_cum.T
    both_e = _dot_f32_by_01(jnp.concatenate([dt, a_cum], axis=0), _head_expand(gw))
    dt_e, acum_e = both_e[:n], both_e[n:]
    dtx = dt_e * xs
    w_end = (jnp.exp(acum_e[n - 1:n, :] - acum_e) * dtx).astype(BF16)
    dtx_b = dtx.astype(BF16)
    exp_acum = jnp.exp(acum_e)
    cb = lax.dot_general(cm, bm, NT_DIMS, preferred_element_type=F32)
    h_in = state_t[...]
    y_inter = jnp.dot(cm, h_in.astype(BF16), preferred_element_type=F32) * exp_acum
    new = lax.dot_general(bm, w_end, TN_DIMS, preferred_element_type=F32)
    state_t[...] = h_in * exp_acum[n - 1:n, :] + new
    for j in range(hpg):
        cols = slice(j * HEAD_P, (j + 1) * HEAD_P)
        seg = a_cum[:, j:j + 1] - a_cum_t[j:j + 1, :]
        decay = jnp.exp(jnp.where(tril, seg, -jnp.inf))
        ybuf[:, cols] = jnp.dot((cb * decay).astype(BF16), dtx_b[:, cols], preferred_element_type=F32)
    y = ybuf[...] + y_inter + dskip_ref[...] * xs
    y_ref[...] = _gated_group_norm(y, z_ref[...], ng_ref[...])

    @pl.when(ci == n_chunks - 1)
    def _():
        final = state_t[...].T
        for j in range(hpg):
            h_out[j] = final[j * HEAD_P:(j + 1) * HEAD_P, :]


def _ssd_prompt(p, dtp, y_mix, bp, seq, d, cw8, cb1, dtb_g, alog_g, dskip_e, ng):
    g = N_GROUPS
    gw = d // g
    hpg = gw // HEAD_P
    n = SSD_CHUNK
    nc = seq // n
    tail = CONV_W - 1
    b_off = d // D_STATE
    c_off = b_off + g
    z_off = 2 * d // gw
    x_off = 3 * d // gw
    pb_off = 4 * d // D_STATE
    pc_off = pb_off + g

    def rows(width, off):
        return pl.BlockSpec((n, width), lambda b, gi, c: (b * nc + c, off + gi))

    def par(r, width, off):
        return pl.BlockSpec((r, width), lambda b, gi, c: (0, off + gi))

    def grp():
        return pl.BlockSpec((None, 1, LANE), lambda b, gi, c: (gi, 0, 0))

    def conv_out(width):
        return pl.BlockSpec((None, tail, width), lambda b, gi, c: (b, 0, gi))

    outs = pl.pallas_call(
        functools.partial(_ssd_p_kernel, hpg=hpg, n_chunks=nc),
        grid=(bp, g, nc),
        in_specs=[
            rows(gw, x_off), rows(D_STATE, pb_off), rows(D_STATE, pc_off), rows(gw, z_off), rows(LANE, 0),
            par(SUBLANE, gw, 0), par(SUBLANE, D_STATE, b_off), par(SUBLANE, D_STATE, c_off),
            par(1, gw, 0), par(1, D_STATE, b_off), par(1, D_STATE, c_off),
            grp(), grp(), par(1, gw, 0), par(1, gw, 0),
            pl.BlockSpec(memory_space=pl.ANY),
        ],
        out_specs=[
            pl.BlockSpec((n, gw), lambda b, gi, c: (b * nc + c, g + gi)),
            conv_out(gw), conv_out(D_STATE), conv_out(D_STATE),
            pl.BlockSpec((None, hpg, HEAD_P, D_STATE), lambda b, gi, c: (b, gi, 0, 0)),
        ],
        input_output_aliases={15: 0},
        out_shape=[
            jax.ShapeDtypeStruct(y_mix.shape, BF16),
            jax.ShapeDtypeStruct((bp, tail, d), F32),
            jax.ShapeDtypeStruct((bp, tail, g * D_STATE), F32),
            jax.ShapeDtypeStruct((bp, tail, g * D_STATE), F32),
            jax.ShapeDtypeStruct((bp, g * hpg, HEAD_P, D_STATE), F32),
        ],
        scratch_shapes=[
            pltpu.VMEM((SUBLANE + n, gw), F32),
            pltpu.VMEM((SUBLANE + n, D_STATE), F32),
            pltpu.VMEM((SUBLANE + n, D_STATE), F32),
            pltpu.VMEM((D_STATE, gw), F32),
            pltpu.VMEM((n, gw), F32),
        ],
        compiler_params=_params("arbitrary", "arbitrary", "arbitrary"),
        name="ssd_prompt",
    )(p, p, p, p, dtp, cw8, cw8, cw8, cb1, cb1, cb1, dtb_g, alog_g, dskip_e, ng, y_mix)
    y_mix, cx, cb_, cc, h_t = outs
    return y_mix, jnp.concatenate([cx, cb_, cc], axis=-1), h_t


def _ssd_s_kernel(x_ref, b_ref, c_ref, z_ref, dt_ref, sx_ref, sb_ref, sc_ref,
                  cwx_ref, cwb_ref, cwc_ref, cbx_ref, cbb_ref, cbc_ref,
                  dtb_ref, alog_ref, dskip_ref, ng_ref, h0_ref, *rest, t_len, sb, chained):
    y_ref, cx_out, cb_out, cc_out, h_out, cbuf, bbuf, wbuf, yibuf = rest[1:] if chained else rest
    tail = CONV_W - 1

    def conv_act(src_ref, st_ref, cw_ref, cb_ref, out_ref):
        full = [st_ref[r] for r in range(tail)] + [src_ref[t] for t in range(t_len)]
        acts = []
        for t in range(t_len):
            acc = cb_ref[...] + cw_ref[0:1, :] * full[t]
            for k in range(1, CONV_W):
                acc = acc + cw_ref[k:k + 1, :] * full[t + k]
            acts.append(_silu(acc))
        for r in range(tail):
            out_ref[r] = full[t_len + r]
        return acts

    xs = conv_act(x_ref, sx_ref, cwx_ref, cbx_ref, cx_out)
    bm = conv_act(b_ref, sb_ref, cwb_ref, cbb_ref, cb_out)
    cm = conv_act(c_ref, sc_ref, cwc_ref, cbc_ref, cc_out)
    gw = xs[0].shape[1]

    a_neg = -jnp.exp(alog_ref[...])
    dts, acums = [], []
    run = None
    for t in range(t_len):
        dt = jax.nn.softplus(dt_ref[t] + dtb_ref[...])
        run = dt * a_neg if run is None else run + dt * a_neg
        dts.append(dt)
        acums.append(run)
    both_e = _dot_f32_by_01(jnp.concatenate(dts + acums, axis=0), _head_expand(gw))
    acum_e = [both_e[(t_len + t) * sb:(t_len + t + 1) * sb] for t in range(t_len)]
    dtx = [both_e[t * sb:(t + 1) * sb] * xs[t] for t in range(t_len)]
    a_last = acum_e[t_len - 1]

    n_panels = gw // LANE

    def put_rows(buf, row0, val):
        for p in range(n_panels):
            buf[p, pl.ds(row0, sb), :] = val[:, p * LANE:(p + 1) * LANE]

    def seq_rows(buf, q):
        return jnp.concatenate([buf[p, pl.ds(q, SUBLANE, stride=sb), :] for p in range(n_panels)], axis=1)

    zero_rows = (SUBLANE - t_len) * sb
    for buf in (cbuf, bbuf):
        buf[pl.ds(t_len * sb, zero_rows), :] = jnp.zeros((zero_rows, D_STATE), F32)
    wbuf[:, pl.ds(t_len * sb, zero_rows), :] = jnp.zeros((n_panels, zero_rows, LANE), F32)
    for t in range(t_len):
        cbuf[pl.ds(t * sb, sb), :] = cm[t]
        bbuf[pl.ds(t * sb, sb), :] = bm[t]
        put_rows(wbuf, t * sb, jnp.exp(a_last - acum_e[t]) * dtx[t])
    chunk_decay = jnp.exp(a_last)
    ones = jnp.ones((SUBLANE, D_STATE), BF16)
    for q in range(sb):
        c_q = cbuf[pl.ds(q, SUBLANE, stride=sb), :].astype(BF16)
        b_q = bbuf[pl.ds(q, SUBLANE, stride=sb), :].astype(BF16)
        w_q = seq_rows(wbuf, q).astype(BF16)
        h0 = h0_ref[q]
        y_q = lax.dot_general(c_q, h0.astype(BF16), NT_DIMS, preferred_element_type=F32)
        for p in range(n_panels):
            yibuf[p, pl.ds(q, SUBLANE, stride=sb), :] = y_q[:, p * LANE:(p + 1) * LANE]
        new = lax.dot_general(w_q, b_q, TN_DIMS, preferred_element_type=F32)
        cd = chunk_decay[q:q + 1, :]
        hi = cd.astype(BF16).astype(F32)
        cd8 = jnp.concatenate([hi, cd - hi, jnp.zeros((SUBLANE - 2, gw), F32)], axis=0).astype(BF16)
        cd_col = lax.dot_general(cd8, ones, TN_DIMS, preferred_element_type=F32)
        h_out[q] = h0 * cd_col + new

    for t in range(t_len):
        y = None
        for s in range(t + 1):
            cbts = jnp.sum(cm[t] * bm[s], axis=-1, keepdims=True)
            term = cbts * jnp.exp(acum_e[t] - acum_e[s]) * dtx[s]
            y = term if y is None else y + term
        y_inter = jnp.concatenate([yibuf[p, pl.ds(t * sb, sb), :] for p in range(n_panels)], axis=1)
        y = y + y_inter * jnp.exp(acum_e[t]) + dskip_ref[...] * xs[t]
        y_ref[t] = _gated_group_norm(y, z_ref[t], ng_ref[...])


def _ssd_sample(p3, dt3, conv3, state_ssm4, ssm_prev, layer, d, cw8, cb1, dtb_g, alog_g, dskip_e, ng):
    t_len, bs, _ = p3.shape
    depth = state_ssm4.shape[0]
    assert t_len <= SUBLANE
    g = N_GROUPS
    gw = d // g
    tail = CONV_W - 1
    sb = SUBLANE
    b_off = d // D_STATE
    c_off = b_off + g
    z_off = 2 * d // gw
    x_off = 3 * d // gw
    pb_off = 4 * d // D_STATE
    pc_off = pb_off + g
    chained = ssm_prev is not None

    def slab(r, width, off):
        return pl.BlockSpec((r, sb, width), lambda i, gi: (0, i, off + gi))

    def par(r, width, off):
        return pl.BlockSpec((r, width), lambda i, gi: (0, off + gi))

    def grp():
        return pl.BlockSpec((None, 1, LANE), lambda i, gi: (gi, 0, 0))

    state_spec = pl.BlockSpec((None, sb, gw, D_STATE), lambda i, gi: (layer, i, gi, 0))
    in_specs = [
        slab(t_len, gw, x_off), slab(t_len, D_STATE, pb_off), slab(t_len, D_STATE, pc_off),
        slab(t_len, gw, z_off), slab(t_len, LANE, 0),
        slab(tail, gw, 0), slab(tail, D_STATE, b_off), slab(tail, D_STATE, c_off),
        par(SUBLANE, gw, 0), par(SUBLANE, D_STATE, b_off), par(SUBLANE, D_STATE, c_off),
        par(1, gw, 0), par(1, D_STATE, b_off), par(1, D_STATE, c_off),
        grp(), grp(), par(1, gw, 0), par(1, gw, 0),
        state_spec,
    ]
    args = [p3, p3, p3, p3, dt3, conv3, conv3, conv3, cw8, cw8, cw8, cb1, cb1, cb1,
            dtb_g, alog_g, dskip_e, ng, state_ssm4]
    if chained:
        in_specs.append(pl.BlockSpec(memory_space=pl.ANY))
        args.append(ssm_prev)
    outs = pl.pallas_call(
        functools.partial(_ssd_s_kernel, t_len=t_len, sb=sb, chained=chained),
        grid=(bs // sb, g),
        in_specs=in_specs,
        out_specs=[
            slab(t_len, gw, 0), slab(tail, gw, 0), slab(tail, D_STATE, 0), slab(tail, D_STATE, 0),
            state_spec,
        ],
        input_output_aliases={len(args) - 1: 4} if chained else {},
        out_shape=[
            jax.ShapeDtypeStruct((t_len, bs, d), BF16),
            jax.ShapeDtypeStruct((tail, bs, d), F32),
            jax.ShapeDtypeStruct((tail, bs, g * D_STATE), F32),
            jax.ShapeDtypeStruct((tail, bs, g * D_STATE), F32),
            jax.ShapeDtypeStruct((depth, bs, d, D_STATE), F32),
        ],
        scratch_shapes=[
            pltpu.VMEM((SUBLANE * sb, D_STATE), F32),
            pltpu.VMEM((SUBLANE * sb, D_STATE), F32),
            pltpu.VMEM((gw // LANE, SUBLANE * sb, LANE), F32),
            pltpu.VMEM((gw // LANE, SUBLANE * sb, LANE), F32),
        ],
        compiler_params=_params("arbitrary", "arbitrary"),
        name="ssd_sample",
    )(*args)
    yb, cx, cb_, cc, h_t = outs
    return yb, jnp.concatenate([cx, cb_, cc], axis=-1), h_t


def _group_pad(v):
    hpg = v.shape[0] // N_GROUPS
    return jnp.pad(v.reshape(N_GROUPS, hpg), ((0, 0), (0, LANE - hpg))).reshape(N_GROUPS, 1, LANE)


def kernel(x_prompt, x_sample, state_conv, state_ssm, c_prompt, c_sample, w_ada, b_ada, norm_pre, norm_post,
           ffn_w_in, ffn_w_out, w_in_mix, w_out_mix, gm_ln_g, gm_ln_b, gm_ws, gm_bs, conv_w, conv_b,
           dt_bias, a_log, d_skip, ssm_norm_g):
    bp, seq, d = x_prompt.shape
    bs, t_len, _ = x_sample.shape
    depth = w_ada.shape[0]
    f = ffn_w_out.shape[2]
    h_b = dt_bias.shape[1]
    conv_dim = conv_w.shape[2]
    g = N_GROUPS
    hpg = h_b // g
    tail = CONV_W - 1
    rows = _Rows(bp, seq, bs, t_len, d)
    tp = rows.tp
    fp = -(-f // 1024) * 1024
    n_proj = 3 * d + conv_dim

    x = jnp.concatenate([x_prompt.reshape(tp, d), x_sample.transpose(1, 0, 2).reshape(rows.ts, d)], axis=0)
    mc = -(-(bs + bp) // BF16_SUBLANE) * BF16_SUBLANE
    c_all = jnp.concatenate([c_sample, c_prompt, jnp.zeros((mc - bs - bp, d), F32)], axis=0)
    mod = _ada(c_all, w_ada, b_ada)
    mod_p = mod[:, bs:bs + bp].reshape(depth * bp * 3 * N_SUB, 1, d)
    npre = norm_pre.reshape(depth * N_SUB, 1, d)
    npost = norm_post.reshape(depth * N_SUB, 1, d)
    state_ssm4 = state_ssm.reshape(depth, bs, h_b * HEAD_P, D_STATE)

    def ffn(h, layer, i):
        a = _glu(h, ffn_w_in, layer, i, f, fp)
        return _mm_acc(a, _cast_rows(ffn_w_out, (layer, i), fp))

    outs = {k: [] for k in ("p_conv", "p_ssm", "s_conv", "s_v")}
    ssm_s = None
    h = _pre(rows, x, npre, mod_p, mod, 0, 0)
    for layer in range(depth):
        x, h = _post(rows, x, ffn(h, layer, 0), npre, npost, mod_p, mod, layer, 0, FFN_RES, (layer, 1))

        p = _mm_w32(h, w_in_mix, layer, n_proj)
        w_dt = _tail_cols(w_in_mix, layer, n_proj)[:, :h_b].reshape(d, g, hpg)
        w_dtp = jnp.pad(w_dt, ((0, 0), (0, 0), (0, LANE - hpg))).reshape(1, d, g * LANE)
        dtp = _mm_w32(h, w_dtp, 0, g * LANE)
        p3 = p[tp:].reshape(t_len, bs, n_proj)
        dt3 = dtp[tp:].reshape(t_len, bs, g * LANE)

        ln_g = gm_ln_g[layer].reshape(1, d)
        ln_b = gm_ln_b[layer].reshape(1, d)
        bs_exp = jnp.repeat(gm_bs[layer].T, LANE, axis=1)
        y_mix = _gmlp_prompt(p, tp, d, ln_g, ln_b, gm_ws[layer], bs_exp)
        wsx = jnp.repeat(gm_ws[layer][:, :t_len, :t_len].transpose(1, 2, 0), LANE, axis=2)
        bsx = jnp.repeat(gm_bs[layer][:, :t_len].T, LANE, axis=1).reshape(t_len, 1, d)
        ya_s, vn_s = _gmlp_sample(p3, d, ln_g, ln_b, wsx, bsx)

        cw8 = jnp.pad(conv_w[layer], ((0, SUBLANE - CONV_W), (0, 0)))
        cb1 = conv_b[layer].reshape(1, conv_dim)
        dtb_g, alog_g = _group_pad(dt_bias[layer]), _group_pad(a_log[layer])
        dskip_e = jnp.repeat(d_skip[layer], HEAD_P).reshape(1, d)
        ng = ssm_norm_g[layer].reshape(1, d)
        y_mix, conv_p, ssm_p = _ssd_prompt(p, dtp, y_mix, bp, seq, d, cw8, cb1, dtb_g, alog_g, dskip_e, ng)
        conv3 = state_conv[layer].transpose(1, 0, 2)
        yb_s, conv_s, ssm_s = _ssd_sample(p3, dt3, conv3, state_ssm4, ssm_s, layer, d,
                                          cw8, cb1, dtb_g, alog_g, dskip_e, ng)
        y_s = jnp.concatenate([ya_s.reshape(rows.ts, d), yb_s.reshape(rows.ts, d)], axis=1)
        y_mix = lax.dynamic_update_slice(y_mix, y_s, (tp, 0))

        o = _mm_acc(y_mix, _cast_rows(w_out_mix, (layer,), 2 * d))
        x, h = _post(rows, x, o, npre, npost, mod_p, mod, layer, 1, 1.0, (layer, 2))

        nxt = (layer + 1, 0) if layer + 1 < depth else None
        x, h = _post(rows, x, ffn(h, layer, 1), npre, npost, mod_p, mod, layer, 2, FFN_RES, nxt)

        outs["p_conv"].append(conv_p)
        outs["p_ssm"].append(ssm_p)
        outs["s_conv"].append(conv_s.transpose(1, 0, 2))
        outs["s_v"].append(vn_s.transpose(1, 0, 2))

    y_prompt = x[:tp].reshape(bp, seq, d)
    y_sample = x[tp:].reshape(t_len, bs, d).transpose(1, 0, 2)
    sample_ssm = ssm_s.reshape(depth, bs, h_b, HEAD_P, D_STATE)
    return (y_prompt, y_sample, jnp.stack(outs["p_conv"]), jnp.stack(outs["p_ssm"]),
            jnp.stack(outs["s_conv"]), sample_ssm, jnp.stack(outs["s_v"]))
```

```python
import functools

import jax
import jax.numpy as jnp
from jax import lax
from jax.experimental import pallas as pl
from jax.experimental.pallas import tpu as pltpu

F32 = jnp.float32
BF16 = jnp.bfloat16
HIGHEST = lax.Precision.HIGHEST

GM_CHUNK = 128
SSD_CHUNK = 128
HEAD_P = 64
D_STATE = 128
N_GROUPS = 8
CONV_W = 4
N_SUB = 3
FFN_RES = 0.5
EPS = 1e-6

LANE = 128
SUBLANE = 8
BF16_SUBLANE = 16
VMEM_LIMIT_BYTES = 56 * 2**20
ROW_SUB = 16
ROW_UNROLL = 4

NT_DIMS = (((1,), (1,)), ((), ()))
TN_DIMS = (((0,), (0,)), ((), ()))


def _params(*sem):
    return pltpu.CompilerParams(dimension_semantics=sem, vmem_limit_bytes=VMEM_LIMIT_BYTES)


def _largest_divisor(n, cap, mult):
    for d in range(min(cap, n), 0, -1):
        if n % d == 0 and d % mult == 0:
            return d
    raise ValueError(f"no tile for {n} (cap {cap}, multiple of {mult})")


def _silu(x):
    return x * (0.5 * jnp.tanh(0.5 * x) + 0.5)


def _rms(x):
    return x * lax.rsqrt(jnp.mean(x * x, axis=-1, keepdims=True) + EPS)


def _ada_kernel(c_ref, w_ref, b_ref, o_ref):
    a = _silu(c_ref[...]).astype(BF16)
    o_ref[...] = jnp.dot(a, w_ref[...].astype(BF16), preferred_element_type=F32) + b_ref[...]


def _ada(c_all, w_ada, b_ada):
    depth, d, n = w_ada.shape
    mc = c_all.shape[0]
    bn = _largest_divisor(n, 512, LANE)
    return pl.pallas_call(
        _ada_kernel,
        grid=(depth, n // bn),
        in_specs=[
            pl.BlockSpec((mc, d), lambda l, j: (0, 0)),
            pl.BlockSpec((None, d, bn), lambda l, j: (l, 0, j)),
            pl.BlockSpec((None, 1, bn), lambda l, j: (l, 0, j)),
        ],
        out_specs=pl.BlockSpec((None, mc, bn), lambda l, j: (l, 0, j)),
        out_shape=jax.ShapeDtypeStruct((depth, mc, n), F32),
        compiler_params=_params("arbitrary", "arbitrary"),
        name="ada_mod",
    )(c_all, w_ada, b_ada.reshape(depth, 1, n))


def _norm_mod(x, g, scale, shift):
    return _rms(x) * g * (1.0 + scale) + shift


def _pre_kernel(x_ref, g_ref, scp_ref, shp_ref, scs_ref, shs_ref, h_ref, *, n_ptiles, bs):
    i = pl.program_id(0)
    sub = min(ROW_SUB, bs)
    n_sub = x_ref.shape[0] // sub

    @pl.when(i < n_ptiles)
    def _():
        def step(r, carry):
            rows = pl.ds(pl.multiple_of(r * sub, sub), sub)
            h_ref[rows, :] = _norm_mod(x_ref[rows, :], g_ref[...], scp_ref[...], shp_ref[...]).astype(BF16)
            return carry

        lax.fori_loop(0, n_sub, step, 0, unroll=ROW_UNROLL)

    @pl.when(i >= n_ptiles)
    def _():
        def step(r, carry):
            rows = pl.ds(pl.multiple_of(r * sub, sub), sub)
            seqs = pl.ds(pl.multiple_of(lax.rem(r * sub, bs), sub), sub)
            h_ref[rows, :] = _norm_mod(x_ref[rows, :], g_ref[...], scs_ref[seqs, :], shs_ref[seqs, :]).astype(BF16)
            return carry

        lax.fori_loop(0, n_sub, step, 0, unroll=ROW_UNROLL)


def _post_kernel(x_ref, o_ref, gpost_ref, gtp_ref, gts_ref, *rest, res_w, n_ptiles, bs, with_next):
    if with_next:
        gpre_ref, scp_ref, shp_ref, scs_ref, shs_ref, xo_ref, h_ref = rest
    else:
        (xo_ref,) = rest
    i = pl.program_id(0)
    sub = min(ROW_SUB, bs)

    def update(rows, gate, scale, shift):
        xn = x_ref[rows, :] + res_w * gate * (_rms(o_ref[rows, :]) * gpost_ref[...])
        xo_ref[rows, :] = xn
        if with_next:
            h_ref[rows, :] = _norm_mod(xn, gpre_ref[...], scale, shift).astype(BF16)

    @pl.when(i < n_ptiles)
    def _():
        def step(r, carry):
            rows = pl.ds(pl.multiple_of(r * sub, sub), sub)
            if with_next:
                update(rows, gtp_ref[...], scp_ref[...], shp_ref[...])
            else:
                update(rows, gtp_ref[...], None, None)
            return carry

        lax.fori_loop(0, x_ref.shape[0] // sub, step, 0, unroll=ROW_UNROLL)

    @pl.when(i >= n_ptiles)
    def _():
        def step(r, carry):
            rows = pl.ds(pl.multiple_of(r * sub, sub), sub)
            seqs = pl.ds(pl.multiple_of(lax.rem(r * sub, bs), sub), sub)
            if with_next:
                update(rows, gts_ref[seqs, :], scs_ref[seqs, :], shs_ref[seqs, :])
            else:
                update(rows, gts_ref[seqs, :], None, None)
            return carry

        lax.fori_loop(0, x_ref.shape[0] // sub, step, 0, unroll=ROW_UNROLL)


class _Rows:
    def __init__(self, bp, seq, bs, t, d):
        self.bp, self.seq, self.bs, self.t, self.d = bp, seq, bs, t, d
        self.tp, self.ts = bp * seq, bs * t
        self.m = self.tp + self.ts
        ks = [k for k in range(t, 0, -1) if t % k == 0 and seq % (k * bs) == 0 and (k * bs <= 256 or k == 1)]
        if not ks:
            raise ValueError("prompt length must be a multiple of the sample batch")
        self.bm = ks[0] * bs
        self.n_ptiles = self.tp // self.bm
        self.n_tiles = self.m // self.bm
        self.tiles_per_seq = seq // self.bm

    def row_spec(self):
        return pl.BlockSpec((self.bm, self.d), lambda i: (i, 0))

    def vec_spec(self, idx):
        return pl.BlockSpec((None, 1, self.d), lambda i: (idx, 0, 0))

    def modp_spec(self, layer, k):
        bp, tps = self.bp, self.tiles_per_seq
        return pl.BlockSpec(
            (None, 1, self.d),
            lambda i: ((layer * bp + jnp.minimum(i // tps, bp - 1)) * (3 * N_SUB) + k, 0, 0))

    def mods_spec(self, layer, k):
        return pl.BlockSpec((None, self.bs, self.d), lambda i: (layer, 0, k))


def _pre(rows, x, norm_pre, mod_p, mod, layer, sub):
    kern = functools.partial(_pre_kernel, n_ptiles=rows.n_ptiles, bs=rows.bs)
    k_shift, k_scale = sub * 3, sub * 3 + 1
    return pl.pallas_call(
        kern,
        grid=(rows.n_tiles,),
        in_specs=[
            rows.row_spec(),
            rows.vec_spec(layer * N_SUB + sub),
            rows.modp_spec(layer, k_scale), rows.modp_spec(layer, k_shift),
            rows.mods_spec(layer, k_scale), rows.mods_spec(layer, k_shift),
        ],
        out_specs=rows.row_spec(),
        out_shape=jax.ShapeDtypeStruct((rows.m, rows.d), BF16),
        compiler_params=_params("arbitrary"),
        name="pre_norm",
    )(x, norm_pre, mod_p, mod_p, mod, mod)


def _post(rows, x, o, norm_pre, norm_post, mod_p, mod, layer, sub, res_w, nxt):
    with_next = nxt is not None
    kern = functools.partial(_post_kernel, res_w=res_w, n_ptiles=rows.n_ptiles, bs=rows.bs, with_next=with_next)
    k_gate = sub * 3 + 2
    in_specs = [
        rows.row_spec(), rows.row_spec(),
        rows.vec_spec(layer * N_SUB + sub),
        rows.modp_spec(layer, k_gate), rows.mods_spec(layer, k_gate),
    ]
    args = [x, o, norm_post, mod_p, mod]
    out_specs = [rows.row_spec()]
    out_shape = [jax.ShapeDtypeStruct((rows.m, rows.d), F32)]
    if with_next:
        nl, ns = nxt
        in_specs += [
            rows.vec_spec(nl * N_SUB + ns),
            rows.modp_spec(nl, ns * 3 + 1), rows.modp_spec(nl, ns * 3),
            rows.mods_spec(nl, ns * 3 + 1), rows.mods_spec(nl, ns * 3),
        ]
        args += [norm_pre, mod_p, mod_p, mod, mod]
        out_specs.append(rows.row_spec())
        out_shape.append(jax.ShapeDtypeStruct((rows.m, rows.d), BF16))
    res = pl.pallas_call(
        kern,
        grid=(rows.n_tiles,),
        in_specs=in_specs,
        out_specs=out_specs,
        out_shape=out_shape,
        compiler_params=_params("arbitrary"),
        name="post_residual",
    )(*args)
    return (res[0], res[1]) if with_next else (res[0], None)


def _mm_acc_kernel(x_ref, w_ref, o_ref):
    k = pl.program_id(2)

    @pl.when(k == 0)
    def _():
        o_ref[...] = jnp.dot(x_ref[...], w_ref[...], preferred_element_type=F32)

    @pl.when(k > 0)
    def _():
        o_ref[...] += jnp.dot(x_ref[...], w_ref[...], preferred_element_type=F32)


def _mm_acc(x, w):
    m, kd = x.shape
    n = w.shape[1]
    bm = _largest_divisor(m, 1088, BF16_SUBLANE)
    bn = _largest_divisor(n, 2048, LANE)
    bk = _largest_divisor(kd, 1024, LANE)
    return pl.pallas_call(
        _mm_acc_kernel,
        grid=(m // bm, n // bn, kd // bk),
        in_specs=[
            pl.BlockSpec((bm, bk), lambda i, j, k: (i, k)),
            pl.BlockSpec((bk, bn), lambda i, j, k: (k, j)),
        ],
        out_specs=pl.BlockSpec((bm, bn), lambda i, j, k: (i, j)),
        out_shape=jax.ShapeDtypeStruct((m, n), F32),
        compiler_params=_params("arbitrary", "arbitrary", "arbitrary"),
        name="matmul_acc",
    )(x, w)


def _mm_w32_kernel(x_ref, wt_ref, o_ref):
    o_ref[...] = lax.dot_general(x_ref[...], wt_ref[...].astype(BF16), NT_DIMS, preferred_element_type=F32)


def _row_block(m):
    return _largest_divisor(m, 2176, BF16_SUBLANE)


def _mm_w32(x, wt_stack, layer, n_cols, *, bn=512):
    m, kd = x.shape
    bm = _row_block(m)
    bn = _largest_divisor(n_cols, bn, LANE)
    return pl.pallas_call(
        _mm_w32_kernel,
        grid=(m // bm, n_cols // bn),
        in_specs=[
            pl.BlockSpec((bm, kd), lambda i, j: (i, 0), pipeline_mode=pl.Buffered(1)),
            pl.BlockSpec((None, bn, kd), lambda i, j: (layer, j, 0)),
        ],
        out_specs=pl.BlockSpec((bm, bn), lambda i, j: (i, j)),
        out_shape=jax.ShapeDtypeStruct((m, n_cols), F32),
        compiler_params=_params("arbitrary", "arbitrary"),
        name="matmul_w32",
    )(x, wt_stack)


def _glu_kernel(x_ref, wg_ref, wu_ref, o_ref, *, nj_valid):
    j = pl.program_id(1)

    @pl.when(j < nj_valid)
    def _():
        x = x_ref[...]
        g = jnp.dot(x, wg_ref[...].astype(BF16), preferred_element_type=F32)
        u = jnp.dot(x, wu_ref[...].astype(BF16), preferred_element_type=F32)
        o_ref[...] = (_silu(g) * u).astype(BF16)

    @pl.when(j >= nj_valid)
    def _():
        o_ref[...] = jnp.zeros_like(o_ref)


def _glu(x, w_in, layer, i, f, fp):
    m, kd = x.shape
    bm = _row_block(m)
    bn = _largest_divisor(f, 256, LANE)
    assert fp % bn == 0
    nj_valid = f // bn
    last = nj_valid - 1
    return pl.pallas_call(
        functools.partial(_glu_kernel, nj_valid=nj_valid),
        grid=(m // bm, fp // bn),
        in_specs=[
            pl.BlockSpec((bm, kd), lambda r, j: (r, 0), pipeline_mode=pl.Buffered(1)),
            pl.BlockSpec((None, None, kd, bn), lambda r, j: (layer, i, 0, jnp.minimum(j, last))),
            pl.BlockSpec((None, None, kd, bn), lambda r, j: (layer, i, 0, nj_valid + jnp.minimum(j, last))),
        ],
        out_specs=pl.BlockSpec((bm, bn), lambda r, j: (r, j)),
        out_shape=jax.ShapeDtypeStruct((m, fp), BF16),
        compiler_params=_params("arbitrary", "arbitrary"),
        name="ffn_in_glu",
    )(x, w_in, w_in)


def _cast_kernel(w_ref, o_ref, *, n_valid):
    r = pl.program_id(0)

    @pl.when(r < n_valid)
    def _():
        o_ref[...] = w_ref[...].astype(BF16)

    @pl.when(r >= n_valid)
    def _():
        o_ref[...] = jnp.zeros_like(o_ref)


def _cast_rows(w_stack, lead, rows_out):
    kd, n = w_stack.shape[-2:]
    br = _largest_divisor(kd, 256, BF16_SUBLANE)
    assert rows_out % br == 0
    n_valid = kd // br
    none = (None,) * len(lead)
    return pl.pallas_call(
        functools.partial(_cast_kernel, n_valid=n_valid),
        grid=(rows_out // br,),
        in_specs=[pl.BlockSpec(none + (br, n), lambda r: lead + (jnp.minimum(r, n_valid - 1), 0))],
        out_specs=pl.BlockSpec((br, n), lambda r: (r, 0)),
        out_shape=jax.ShapeDtypeStruct((rows_out, n), BF16),
        compiler_params=_params("arbitrary"),
        name="cast_weight",
    )(w_stack)


def _gelu_ln(v, ln_g, ln_b):
    gv = jax.nn.gelu(v)
    mu = jnp.mean(gv, axis=-1, keepdims=True)
    cen = gv - mu
    var = jnp.mean(cen * cen, axis=-1, keepdims=True)
    return cen * lax.rsqrt(var + EPS) * ln_g + ln_b


def _gmlp_p_kernel(u_ref, v_ref, lng_ref, lnb_ref, ws_ref, bs_ref, o_ref, *, n_heads):
    c = GM_CHUNK
    vn = _gelu_ln(v_ref[...], lng_ref[...], lnb_ref[...]).astype(BF16)
    tril = lax.broadcasted_iota(jnp.int32, (c, c), 0) >= lax.broadcasted_iota(jnp.int32, (c, c), 1)
    for h in range(n_heads):
        cols = slice(h * LANE, (h + 1) * LANE)
        w = jnp.where(tril, ws_ref[h], 0.0).astype(BF16)
        f = jnp.dot(w, vn[:, cols], preferred_element_type=F32) + bs_ref[:, cols]
        o_ref[:, cols] = (jax.nn.gelu(u_ref[:, cols]) * f).astype(BF16)


def _gmlp_prompt(p, tp, d, ln_g, ln_b, ws, bs_exp):
    n_heads = d // LANE
    c = GM_CHUNK
    m = p.shape[0]
    return pl.pallas_call(
        functools.partial(_gmlp_p_kernel, n_heads=n_heads),
        grid=(tp // c,),
        in_specs=[
            pl.BlockSpec((c, d), lambda i: (i, 0)),
            pl.BlockSpec((c, d), lambda i: (i, 1)),
            pl.BlockSpec((1, d), lambda i: (0, 0)),
            pl.BlockSpec((1, d), lambda i: (0, 0)),
            pl.BlockSpec((n_heads, c, c), lambda i: (0, 0, 0)),
            pl.BlockSpec((c, d), lambda i: (0, 0)),
        ],
        out_specs=pl.BlockSpec((c, d), lambda i: (i, 0)),
        out_shape=jax.ShapeDtypeStruct((m, 2 * d), BF16),
        compiler_params=_params("arbitrary"),
        name="gmlp_prompt",
    )(p, p, ln_g, ln_b, ws, bs_exp)


def _gmlp_s_kernel(u_ref, v_ref, lng_ref, lnb_ref, wsx_ref, bsx_ref, ya_ref, vn_ref, vnb, facc, *, t_len):
    t = pl.program_id(0)
    vn = _gelu_ln(v_ref[...], lng_ref[...], lnb_ref[...])
    vn_ref[...] = vn
    vnb[t] = vn.astype(BF16)
    facc[...] = jnp.broadcast_to(bsx_ref[t], facc.shape)
    w_t = wsx_ref[t]
    for s in range(t_len):
        @pl.when(s <= t)
        def _():
            w = w_t[s:s + 1, :].astype(BF16).astype(F32)
            facc[...] += w * vnb[s].astype(F32)
    ya_ref[...] = (jax.nn.gelu(u_ref[...]) * facc[...]).astype(BF16)


def _gmlp_sample(uvz3, d, ln_g, ln_b, wsx, bsx):
    t_len, bs, _ = uvz3.shape
    return pl.pallas_call(
        functools.partial(_gmlp_s_kernel, t_len=t_len),
        grid=(t_len,),
        in_specs=[
            pl.BlockSpec((None, bs, d), lambda t: (t, 0, 0)),
            pl.BlockSpec((None, bs, d), lambda t: (t, 0, 1)),
            pl.BlockSpec((1, d), lambda t: (0, 0)),
            pl.BlockSpec((1, d), lambda t: (0, 0)),
            pl.BlockSpec((t_len, t_len, d), lambda t: (0, 0, 0)),
            pl.BlockSpec((t_len, 1, d), lambda t: (0, 0, 0)),
        ],
        out_specs=[
            pl.BlockSpec((None, bs, d), lambda t: (t, 0, 0)),
            pl.BlockSpec((None, bs, d), lambda t: (t, 0, 0)),
        ],
        out_shape=[
            jax.ShapeDtypeStruct((t_len, bs, d), BF16),
            jax.ShapeDtypeStruct((t_len, bs, d), F32),
        ],
        scratch_shapes=[pltpu.VMEM((t_len, bs, d), BF16), pltpu.VMEM((bs, d), F32)],
        compiler_params=_params("arbitrary"),
        name="gmlp_sample",
    )(uvz3, uvz3, ln_g, ln_b, wsx, bsx)


def _head_expand(gw):
    row = lax.broadcasted_iota(jnp.int32, (LANE, gw), 0)
    col = lax.broadcasted_iota(jnp.int32, (LANE, gw), 1)
    return (lax.shift_right_logical(col, 6) == row).astype(BF16)


def _split3(x):
    x1 = x.astype(BF16)
    r1 = x - x1.astype(F32)
    x2 = r1.astype(BF16)
    x3 = (r1 - x2.astype(F32)).astype(BF16)
    return x1, x2, x3


def _dot_f32_by_01(x, m01):
    r = x.shape[0]
    y = jnp.dot(jnp.concatenate(_split3(x), axis=0), m01, preferred_element_type=F32)
    return y[:r] + y[r:2 * r] + y[2 * r:]


def _dot_01_by_f32(m01, x):
    n = x.shape[1]
    y = jnp.dot(m01, jnp.concatenate(_split3(x), axis=1), preferred_element_type=F32)
    return y[:, :n] + y[:, n:2 * n] + y[:, 2 * n:]


def _gated_group_norm(y, z, ng):
    y = y * _silu(z)
    return (_rms(y) * ng).astype(BF16)


def _ssd_p_kernel(x_ref, b_ref, c_ref, z_ref, dt_ref, cwx_ref, cwb_ref, cwc_ref, cbx_ref, cbb_ref, cbc_ref,
                  dtb_ref, alog_ref, dskip_ref, ng_ref, ymix_ref,
                  y_ref, cx_out, cb_out, cc_out, h_out,
                  xpad, bpad, cpad, state_t, ybuf, *, hpg, n_chunks):
    del ymix_ref
    n = SSD_CHUNK
    tail = CONV_W - 1
    first = SUBLANE - tail
    ci = pl.program_id(2)

    @pl.when(ci == 0)
    def _():
        state_t[...] = jnp.zeros_like(state_t)
        for pad in (xpad, bpad, cpad):
            pad[pl.ds(0, SUBLANE), :] = jnp.zeros((SUBLANE, pad.shape[1]), F32)

    def conv_act(src_ref, pad_ref, cw_ref, cb_ref, out_ref):
        x = src_ref[...]
        pad_ref[pl.ds(SUBLANE, n), :] = x
        acc = cb_ref[...] + cw_ref[0:1, :] * pad_ref[pl.ds(first, n), :]
        for k in range(1, tail):
            acc = acc + cw_ref[k:k + 1, :] * pad_ref[pl.ds(first + k, n), :]
        acc = acc + cw_ref[tail:tail + 1, :] * x
        last = src_ref[pl.ds(n - tail, tail), :]
        pad_ref[pl.ds(first, tail), :] = last
        out_ref[...] = last
        return _silu(acc)

    xs = conv_act(x_ref, xpad, cwx_ref, cbx_ref, cx_out)
    bm = conv_act(b_ref, bpad, cwb_ref, cbb_ref, cb_out).astype(BF16)
    cm = conv_act(c_ref, cpad, cwc_ref, cbc_ref, cc_out).astype(BF16)

    gw = xs.shape[1]
    dt = jax.nn.softplus(dt_ref[...] + dtb_ref[...])
    a = dt * (-jnp.exp(alog_ref[...]))
    tril = lax.broadcasted_iota(jnp.int32, (n, n), 0) >= lax.broadcasted_iota(jnp.int32, (n, n), 1)
    a_cum = _dot_01_by_f32(tril.astype(BF16), a)
    a_cum_t = a_cum.T
    both_e = _dot_f32_by_01(jnp.concatenate([dt, a_cum], axis=0), _head_expand(gw))
    dt_e, acum_e = both_e[:n], both_e[n:]
    dtx = dt_e * xs
    w_end = (jnp.exp(acum_e[n - 1:n, :] - acum_e) * dtx).astype(BF16)
    dtx_b = dtx.astype(BF16)
    exp_acum = jnp.exp(acum_e)
    cb = lax.dot_general(cm, bm, NT_DIMS, preferred_element_type=F32)
    h_in = state_t[...]
    y_inter = jnp.dot(cm, h_in.astype(BF16), preferred_element_type=F32) * exp_acum
    new = lax.dot_general(bm, w_end, TN_DIMS, preferred_element_type=F32)
    state_t[...] = h_in * exp_acum[n - 1:n, :] + new
    for j in range(hpg):
        cols = slice(j * HEAD_P, (j + 1) * HEAD_P)
        seg = a_cum[:, j:j + 1] - a_cum_t[j:j + 1, :]
        decay = jnp.exp(jnp.where(tril, seg, -jnp.inf))
        ybuf[:, cols] = jnp.dot((cb * decay).astype(BF16), dtx_b[:, cols], preferred_element_type=F32)
    y = ybuf[...] + y_inter + dskip_ref[...] * xs
    y_ref[...] = _gated_group_norm(y, z_ref[...], ng_ref[...])

    @pl.when(ci == n_chunks - 1)
    def _():
        final = state_t[...].T
        for j in range(hpg):
            h_out[j] = final[j * HEAD_P:(j + 1) * HEAD_P, :]


def _ssd_prompt(p, dtp, y_mix, bp, seq, d, cw8, cb1, dtb_g, alog_g, dskip_e, ng):
    g = N_GROUPS
    gw = d // g
    hpg = gw // HEAD_P
    n = SSD_CHUNK
    nc = seq // n
    tail = CONV_W - 1
    b_off = d // D_STATE
    c_off = b_off + g
    z_off = 2 * d // gw
    x_off = 3 * d // gw
    pb_off = 4 * d // D_STATE
    pc_off = pb_off + g

    def rows(width, off):
        return pl.BlockSpec((n, width), lambda b, gi, c: (b * nc + c, off + gi))

    def par(r, width, off):
        return pl.BlockSpec((r, width), lambda b, gi, c: (0, off + gi))

    def grp():
        return pl.BlockSpec((None, 1, LANE), lambda b, gi, c: (gi, 0, 0))

    def conv_out(width):
        return pl.BlockSpec((None, tail, width), lambda b, gi, c: (b, 0, gi))

    outs = pl.pallas_call(
        functools.partial(_ssd_p_kernel, hpg=hpg, n_chunks=nc),
        grid=(bp, g, nc),
        in_specs=[
            rows(gw, x_off), rows(D_STATE, pb_off), rows(D_STATE, pc_off), rows(gw, z_off), rows(LANE, 0),
            par(SUBLANE, gw, 0), par(SUBLANE, D_STATE, b_off), par(SUBLANE, D_STATE, c_off),
            par(1, gw, 0), par(1, D_STATE, b_off), par(1, D_STATE, c_off),
            grp(), grp(), par(1, gw, 0), par(1, gw, 0),
            pl.BlockSpec(memory_space=pl.ANY),
        ],
        out_specs=[
            pl.BlockSpec((n, gw), lambda b, gi, c: (b * nc + c, g + gi)),
            conv_out(gw), conv_out(D_STATE), conv_out(D_STATE),
            pl.BlockSpec((None, hpg, HEAD_P, D_STATE), lambda b, gi, c: (b, gi, 0, 0)),
        ],
        input_output_aliases={15: 0},
        out_shape=[
            jax.ShapeDtypeStruct(y_mix.shape, BF16),
            jax.ShapeDtypeStruct((bp, tail, d), F32),
            jax.ShapeDtypeStruct((bp, tail, g * D_STATE), F32),
            jax.ShapeDtypeStruct((bp, tail, g * D_STATE), F32),
            jax.ShapeDtypeStruct((bp, g * hpg, HEAD_P, D_STATE), F32),
        ],
        scratch_shapes=[
            pltpu.VMEM((SUBLANE + n, gw), F32),
            pltpu.VMEM((SUBLANE + n, D_STATE), F32),
            pltpu.VMEM((SUBLANE + n, D_STATE), F32),
            pltpu.VMEM((D_STATE, gw), F32),
            pltpu.VMEM((n, gw), F32),
        ],
        compiler_params=_params("arbitrary", "arbitrary", "arbitrary"),
        name="ssd_prompt",
    )(p, p, p, p, dtp, cw8, cw8, cw8, cb1, cb1, cb1, dtb_g, alog_g, dskip_e, ng, y_mix)
    y_mix, cx, cb_, cc, h_t = outs
    return y_mix, jnp.concatenate([cx, cb_, cc], axis=-1), h_t


def _ssd_s_kernel(x_ref, b_ref, c_ref, z_ref, dt_ref, sx_ref, sb_ref, sc_ref,
                  cwx_ref, cwb_ref, cwc_ref, cbx_ref, cbb_ref, cbc_ref,
                  dtb_ref, alog_ref, dskip_ref, ng_ref, h0_ref, *rest, t_len, sb, chained):
    y_ref, cx_out, cb_out, cc_out, h_out, cbuf, bbuf, wbuf, yibuf = rest[1:] if chained else rest
    tail = CONV_W - 1

    def conv_act(src_ref, st_ref, cw_ref, cb_ref, out_ref):
        full = [st_ref[r] for r in range(tail)] + [src_ref[t] for t in range(t_len)]
        acts = []
        for t in range(t_len):
            acc = cb_ref[...] + cw_ref[0:1, :] * full[t]
            for k in range(1, CONV_W):
                acc = acc + cw_ref[k:k + 1, :] * full[t + k]
            acts.append(_silu(acc))
        for r in range(tail):
            out_ref[r] = full[t_len + r]
        return acts

    xs = conv_act(x_ref, sx_ref, cwx_ref, cbx_ref, cx_out)
    bm = conv_act(b_ref, sb_ref, cwb_ref, cbb_ref, cb_out)
    cm = conv_act(c_ref, sc_ref, cwc_ref, cbc_ref, cc_out)
    gw = xs[0].shape[1]

    a_neg = -jnp.exp(alog_ref[...])
    dts, acums = [], []
    run = None
    for t in range(t_len):
        dt = jax.nn.softplus(dt_ref[t] + dtb_ref[...])
        run = dt * a_neg if run is None else run + dt * a_neg
        dts.append(dt)
        acums.append(run)
    both_e = _dot_f32_by_01(jnp.concatenate(dts + acums, axis=0), _head_expand(gw))
    acum_e = [both_e[(t_len + t) * sb:(t_len + t + 1) * sb] for t in range(t_len)]
    dtx = [both_e[t * sb:(t + 1) * sb] * xs[t] for t in range(t_len)]
    a_last = acum_e[t_len - 1]

    n_panels = gw // LANE

    def put_rows(buf, row0, val):
        for p in range(n_panels):
            buf[p, pl.ds(row0, sb), :] = val[:, p * LANE:(p + 1) * LANE]

    def seq_rows(buf, q):
        return jnp.concatenate([buf[p, pl.ds(q, SUBLANE, stride=sb), :] for p in range(n_panels)], axis=1)

    zero_rows = (SUBLANE - t_len) * sb
    for buf in (cbuf, bbuf):
        buf[pl.ds(t_len * sb, zero_rows), :] = jnp.zeros((zero_rows, D_STATE), F32)
    wbuf[:, pl.ds(t_len * sb, zero_rows), :] = jnp.zeros((n_panels, zero_rows, LANE), F32)
    for t in range(t_len):
        cbuf[pl.ds(t * sb, sb), :] = cm[t]
        bbuf[pl.ds(t * sb, sb), :] = bm[t]
        put_rows(wbuf, t * sb, jnp.exp(a_last - acum_e[t]) * dtx[t])
    chunk_decay = jnp.exp(a_last)
    ones = jnp.ones((SUBLANE, D_STATE), BF16)
    for q in range(sb):
        c_q = cbuf[pl.ds(q, SUBLANE, stride=sb), :].astype(BF16)
        b_q = bbuf[pl.ds(q, SUBLANE, stride=sb), :].astype(BF16)
        w_q = seq_rows(wbuf, q).astype(BF16)
        h0 = h0_ref[q]
        y_q = lax.dot_general(c_q, h0.astype(BF16), NT_DIMS, preferred_element_type=F32)
        for p in range(n_panels):
            yibuf[p, pl.ds(q, SUBLANE, stride=sb), :] = y_q[:, p * LANE:(p + 1) * LANE]
        new = lax.dot_general(w_q, b_q, TN_DIMS, preferred_element_type=F32)
        cd = chunk_decay[q:q + 1, :]
        hi = cd.astype(BF16).astype(F32)
        cd8 = jnp.concatenate([hi, cd - hi, jnp.zeros((SUBLANE - 2, gw), F32)], axis=0).astype(BF16)
        cd_col = lax.dot_general(cd8, ones, TN_DIMS, preferred_element_type=F32)
        h_out[q] = h0 * cd_col + new

    for t in range(t_len):
        y = None
        for s in range(t + 1):
            cbts = jnp.sum(cm[t] * bm[s], axis=-1, keepdims=True)
            term = cbts * jnp.exp(acum_e[t] - acum_e[s]) * dtx[s]
            y = term if y is None else y + term
        y_inter = jnp.concatenate([yibuf[p, pl.ds(t * sb, sb), :] for p in range(n_panels)], axis=1)
        y = y + y_inter * jnp.exp(acum_e[t]) + dskip_ref[...] * xs[t]
        y_ref[t] = _gated_group_norm(y, z_ref[t], ng_ref[...])


def _ssd_sample(p3, dt3, conv3, state_ssm4, ssm_prev, layer, d, cw8, cb1, dtb_g, alog_g, dskip_e, ng):
    t_len, bs, _ = p3.shape
    depth = state_ssm4.shape[0]
    assert t_len <= SUBLANE
    g = N_GROUPS
    gw = d // g
    tail = CONV_W - 1
    sb = SUBLANE
    b_off = d // D_STATE
    c_off = b_off + g
    z_off = 2 * d // gw
    x_off = 3 * d // gw
    pb_off = 4 * d // D_STATE
    pc_off = pb_off + g
    chained = ssm_prev is not None

    def slab(r, width, off):
        return pl.BlockSpec((r, sb, width), lambda i, gi: (0, i, off + gi))

    def par(r, width, off):
        return pl.BlockSpec((r, width), lambda i, gi: (0, off + gi))

    def grp():
        return pl.BlockSpec((None, 1, LANE), lambda i, gi: (gi, 0, 0))

    state_spec = pl.BlockSpec((None, sb, gw, D_STATE), lambda i, gi: (layer, i, gi, 0))
    in_specs = [
        slab(t_len, gw, x_off), slab(t_len, D_STATE, pb_off), slab(t_len, D_STATE, pc_off),
        slab(t_len, gw, z_off), slab(t_len, LANE, 0),
        slab(tail, gw, 0), slab(tail, D_STATE, b_off), slab(tail, D_STATE, c_off),
        par(SUBLANE, gw, 0), par(SUBLANE, D_STATE, b_off), par(SUBLANE, D_STATE, c_off),
        par(1, gw, 0), par(1, D_STATE, b_off), par(1, D_STATE, c_off),
        grp(), grp(), par(1, gw, 0), par(1, gw, 0),
        state_spec,
    ]
    args = [p3, p3, p3, p3, dt3, conv3, conv3, conv3, cw8, cw8, cw8, cb1, cb1, cb1,
            dtb_g, alog_g, dskip_e, ng, state_ssm4]
    if chained:
        in_specs.append(pl.BlockSpec(memory_space=pl.ANY))
        args.append(ssm_prev)
    outs = pl.pallas_call(
        functools.partial(_ssd_s_kernel, t_len=t_len, sb=sb, chained=chained),
        grid=(bs // sb, g),
        in_specs=in_specs,
        out_specs=[
            slab(t_len, gw, 0), slab(tail, gw, 0), slab(tail, D_STATE, 0), slab(tail, D_STATE, 0),
            state_spec,
        ],
        input_output_aliases={len(args) - 1: 4} if chained else {},
        out_shape=[
            jax.ShapeDtypeStruct((t_len, bs, d), BF16),
            jax.ShapeDtypeStruct((tail, bs, d), F32),
            jax.ShapeDtypeStruct((tail, bs, g * D_STATE), F32),
            jax.ShapeDtypeStruct((tail, bs, g * D_STATE), F32),
            jax.ShapeDtypeStruct((depth, bs, d, D_STATE), F32),
        ],
        scratch_shapes=[
            pltpu.VMEM((SUBLANE * sb, D_STATE), F32),
            pltpu.VMEM((SUBLANE * sb, D_STATE), F32),
            pltpu.VMEM((gw // LANE, SUBLANE * sb, LANE), F32),
            pltpu.VMEM((gw // LANE, SUBLANE * sb, LANE), F32),
        ],
        compiler_params=_params("arbitrary", "arbitrary"),
        name="ssd_sample",
    )(*args)
    yb, cx, cb_, cc, h_t = outs
    return yb, jnp.concatenate([cx, cb_, cc], axis=-1), h_t


def _group_pad(v):
    hpg = v.shape[0] // N_GROUPS
    return jnp.pad(v.reshape(N_GROUPS, hpg), ((0, 0), (0, LANE - hpg))).reshape(N_GROUPS, 1, LANE)


def kernel(x_prompt, x_sample, state_conv, state_ssm, c_prompt, c_sample, w_ada, b_ada, norm_pre, norm_post,
           ffn_w_in, ffn_w_out, w_in_mix, w_out_mix, gm_ln_g, gm_ln_b, gm_ws, gm_bs, conv_w, conv_b,
           dt_bias, a_log, d_skip, ssm_norm_g):
    bp, seq, d = x_prompt.shape
    bs, t_len, _ = x_sample.shape
    depth = w_ada.shape[0]
    f = ffn_w_out.shape[2]
    h_b = dt_bias.shape[1]
    conv_dim = conv_w.shape[2]
    g = N_GROUPS
    hpg = h_b // g
    tail = CONV_W - 1
    rows = _Rows(bp, seq, bs, t_len, d)
    tp = rows.tp
    fp = -(-f // 1024) * 1024
    n_proj = 3 * d + conv_dim

    x = jnp.concatenate([x_prompt.reshape(tp, d), x_sample.transpose(1, 0, 2).reshape(rows.ts, d)], axis=0)
    mc = -(-(bs + bp) // BF16_SUBLANE) * BF16_SUBLANE
    c_all = jnp.concatenate([c_sample, c_prompt, jnp.zeros((mc - bs - bp, d), F32)], axis=0)
    mod = _ada(c_all, w_ada, b_ada)
    mod_p = mod[:, bs:bs + bp].reshape(depth * bp * 3 * N_SUB, 1, d)
    npre = norm_pre.reshape(depth * N_SUB, 1, d)
    npost = norm_post.reshape(depth * N_SUB, 1, d)
    state_ssm4 = state_ssm.reshape(depth, bs, h_b * HEAD_P, D_STATE)

    def ffn(h, layer, i):
        a = _glu(h, ffn_w_in, layer, i, f, fp)
        return _mm_acc(a, _cast_rows(ffn_w_out, (layer, i), fp))

    w_mix_t = jnp.swapaxes(w_in_mix, 1, 2)
    outs = {k: [] for k in ("p_conv", "p_ssm", "s_conv", "s_v")}
    ssm_s = None
    h = _pre(rows, x, npre, mod_p, mod, 0, 0)
    for layer in range(depth):
        x, h = _post(rows, x, ffn(h, layer, 0), npre, npost, mod_p, mod, layer, 0, FFN_RES, (layer, 1))

        p = _mm_w32(h, w_mix_t, layer, n_proj)
        w_dt_t = w_mix_t[layer, n_proj:].reshape(g, hpg, d)
        w_dtp_t = jnp.pad(w_dt_t, ((0, 0), (0, LANE - hpg), (0, 0))).reshape(1, g * LANE, d)
        dtp = _mm_w32(h, w_dtp_t, 0, g * LANE)
        p3 = p[tp:].reshape(t_len, bs, n_proj)
        dt3 = dtp[tp:].reshape(t_len, bs, g * LANE)

        ln_g = gm_ln_g[layer].reshape(1, d)
        ln_b = gm_ln_b[layer].reshape(1, d)
        bs_exp = jnp.repeat(gm_bs[layer].T, LANE, axis=1)
        y_mix = _gmlp_prompt(p, tp, d, ln_g, ln_b, gm_ws[layer], bs_exp)
        wsx = jnp.repeat(gm_ws[layer][:, :t_len, :t_len].transpose(1, 2, 0), LANE, axis=2)
        bsx = jnp.repeat(gm_bs[layer][:, :t_len].T, LANE, axis=1).reshape(t_len, 1, d)
        ya_s, vn_s = _gmlp_sample(p3, d, ln_g, ln_b, wsx, bsx)

        cw8 = jnp.pad(conv_w[layer], ((0, SUBLANE - CONV_W), (0, 0)))
        cb1 = conv_b[layer].reshape(1, conv_dim)
        dtb_g, alog_g = _group_pad(dt_bias[layer]), _group_pad(a_log[layer])
        dskip_e = jnp.repeat(d_skip[layer], HEAD_P).reshape(1, d)
        ng = ssm_norm_g[layer].reshape(1, d)
        y_mix, conv_p, ssm_p = _ssd_prompt(p, dtp, y_mix, bp, seq, d, cw8, cb1, dtb_g, alog_g, dskip_e, ng)
        conv3 = state_conv[layer].transpose(1, 0, 2)
        yb_s, conv_s, ssm_s = _ssd_sample(p3, dt3, conv3, state_ssm4, ssm_s, layer, d,
                                          cw8, cb1, dtb_g, alog_g, dskip_e, ng)
        y_s = jnp.concatenate([ya_s.reshape(rows.ts, d), yb_s.reshape(rows.ts, d)], axis=1)
        y_mix = lax.dynamic_update_slice(y_mix, y_s, (tp, 0))

        o = _mm_acc(y_mix, _cast_rows(w_out_mix, (layer,), 2 * d))
        x, h = _post(rows, x, o, npre, npost, mod_p, mod, layer, 1, 1.0, (layer, 2))

        nxt = (layer + 1, 0) if layer + 1 < depth else None
        x, h = _post(rows, x, ffn(h, layer, 1), npre, npost, mod_p, mod, layer, 2, FFN_RES, nxt)

        outs["p_conv"].append(conv_p)
        outs["p_ssm"].append(ssm_p)
        outs["s_conv"].append(conv_s.transpose(1, 0, 2))
        outs["s_v"].append(vn_s.transpose(1, 0, 2))

    y_prompt = x[:tp].reshape(bp, seq, d)
    y_sample = x[tp:].reshape(t_len, bs, d).transpose(1, 0, 2)
    sample_ssm = ssm_s.reshape(depth, bs, h_b, HEAD_P, D_STATE)
    return (y_prompt, y_sample, jnp.stack(outs["p_conv"]), jnp.stack(outs["p_ssm"]),
            jnp.stack(outs["s_conv"]), sample_ssm, jnp.stack(outs["s_v"]))
```

```python
import functools

import jax
import jax.numpy as jnp
from jax import lax
from jax.experimental import pallas as pl
from jax.experimental.pallas import tpu as pltpu

F32 = jnp.float32
BF16 = jnp.bfloat16

GM_CHUNK = 128
SSD_CHUNK = 128
HEAD_P = 64
D_STATE = 128
N_GROUPS = 8
CONV_W = 4
N_SUB = 3
FFN_RES = 0.5
EPS = 1e-6

LANE = 128
SUBLANE = 8
BF16_SUBLANE = 16
VMEM_LIMIT_BYTES = 56 * 2**20
ROW_SUB = 16
ROW_UNROLL = 4

NT_DIMS = (((1,), (1,)), ((), ()))
TN_DIMS = (((0,), (0,)), ((), ()))


def _params(*sem):
    return pltpu.CompilerParams(dimension_semantics=sem, vmem_limit_bytes=VMEM_LIMIT_BYTES)


def _largest_divisor(n, cap, mult):
    for d in range(min(cap, n), 0, -1):
        if n % d == 0 and d % mult == 0:
            return d
    raise ValueError(f"no tile for {n} (cap {cap}, multiple of {mult})")


def _silu(x):
    return x * (0.5 * jnp.tanh(0.5 * x) + 0.5)


def _rms(x):
    return x * lax.rsqrt(jnp.mean(x * x, axis=-1, keepdims=True) + EPS)


def _ada_kernel(c_ref, w_ref, b_ref, o_ref):
    a = _silu(c_ref[...]).astype(BF16)
    o_ref[...] = jnp.dot(a, w_ref[...].astype(BF16), preferred_element_type=F32) + b_ref[...]


def _ada(c_all, w_ada, b_ada):
    depth, d, n = w_ada.shape
    mc = c_all.shape[0]
    bn = _largest_divisor(n, 512, LANE)
    return pl.pallas_call(
        _ada_kernel,
        grid=(depth, n // bn),
        in_specs=[
            pl.BlockSpec((mc, d), lambda l, j: (0, 0)),
            pl.BlockSpec((None, d, bn), lambda l, j: (l, 0, j)),
            pl.BlockSpec((None, 1, bn), lambda l, j: (l, 0, j)),
        ],
        out_specs=pl.BlockSpec((None, mc, bn), lambda l, j: (l, 0, j)),
        out_shape=jax.ShapeDtypeStruct((depth, mc, n), F32),
        compiler_params=_params("arbitrary", "arbitrary"),
        name="ada_mod",
    )(c_all, w_ada, b_ada.reshape(depth, 1, n))


def _norm_mod(x, g, scale, shift):
    return _rms(x) * g * (1.0 + scale) + shift


def _pre_kernel(x_ref, g_ref, scp_ref, shp_ref, scs_ref, shs_ref, h_ref, *, n_ptiles, bs):
    i = pl.program_id(0)
    sub = min(ROW_SUB, bs)
    n_sub = x_ref.shape[0] // sub

    @pl.when(i < n_ptiles)
    def _():
        def step(r, carry):
            rows = pl.ds(pl.multiple_of(r * sub, sub), sub)
            h_ref[rows, :] = _norm_mod(x_ref[rows, :], g_ref[...], scp_ref[...], shp_ref[...]).astype(BF16)
            return carry

        lax.fori_loop(0, n_sub, step, 0, unroll=ROW_UNROLL)

    @pl.when(i >= n_ptiles)
    def _():
        def step(r, carry):
            rows = pl.ds(pl.multiple_of(r * sub, sub), sub)
            seqs = pl.ds(pl.multiple_of(lax.rem(r * sub, bs), sub), sub)
            h_ref[rows, :] = _norm_mod(x_ref[rows, :], g_ref[...], scs_ref[seqs, :], shs_ref[seqs, :]).astype(BF16)
            return carry

        lax.fori_loop(0, n_sub, step, 0, unroll=ROW_UNROLL)


def _post_kernel(x_ref, o_ref, gpost_ref, gtp_ref, gts_ref, *rest, res_w, n_ptiles, bs, with_next):
    if with_next:
        gpre_ref, scp_ref, shp_ref, scs_ref, shs_ref, xo_ref, h_ref = rest
    else:
        (xo_ref,) = rest
    i = pl.program_id(0)
    sub = min(ROW_SUB, bs)

    def update(rows, gate, scale, shift):
        xn = x_ref[rows, :] + res_w * gate * (_rms(o_ref[rows, :]) * gpost_ref[...])
        xo_ref[rows, :] = xn
        if with_next:
            h_ref[rows, :] = _norm_mod(xn, gpre_ref[...], scale, shift).astype(BF16)

    @pl.when(i < n_ptiles)
    def _():
        def step(r, carry):
            rows = pl.ds(pl.multiple_of(r * sub, sub), sub)
            if with_next:
                update(rows, gtp_ref[...], scp_ref[...], shp_ref[...])
            else:
                update(rows, gtp_ref[...], None, None)
            return carry

        lax.fori_loop(0, x_ref.shape[0] // sub, step, 0, unroll=ROW_UNROLL)

    @pl.when(i >= n_ptiles)
    def _():
        def step(r, carry):
            rows = pl.ds(pl.multiple_of(r * sub, sub), sub)
            seqs = pl.ds(pl.multiple_of(lax.rem(r * sub, bs), sub), sub)
            if with_next:
                update(rows, gts_ref[seqs, :], scs_ref[seqs, :], shs_ref[seqs, :])
            else:
                update(rows, gts_ref[seqs, :], None, None)
            return carry

        lax.fori_loop(0, x_ref.shape[0] // sub, step, 0, unroll=ROW_UNROLL)


class _Rows:
    def __init__(self, bp, seq, bs, t, d):
        self.bp, self.seq, self.bs, self.t, self.d = bp, seq, bs, t, d
        self.tp, self.ts = bp * seq, bs * t
        self.m = self.tp + self.ts
        ks = [k for k in range(t, 0, -1) if t % k == 0 and seq % (k * bs) == 0 and (k * bs <= 256 or k == 1)]
        if not ks:
            raise ValueError("prompt length must be a multiple of the sample batch")
        self.bm = ks[0] * bs
        self.n_ptiles = self.tp // self.bm
        self.n_tiles = self.m // self.bm
        self.tiles_per_seq = seq // self.bm

    def row_spec(self):
        return pl.BlockSpec((self.bm, self.d), lambda i: (i, 0))

    def vec_spec(self, idx):
        return pl.BlockSpec((None, 1, self.d), lambda i: (idx, 0, 0))

    def modp_spec(self, layer, k):
        bp, tps = self.bp, self.tiles_per_seq
        return pl.BlockSpec(
            (None, 1, self.d),
            lambda i: ((layer * bp + jnp.minimum(i // tps, bp - 1)) * (3 * N_SUB) + k, 0, 0))

    def mods_spec(self, layer, k):
        return pl.BlockSpec((None, self.bs, self.d), lambda i: (layer, 0, k))


def _pre(rows, x, norm_pre, mod_p, mod, layer, sub):
    kern = functools.partial(_pre_kernel, n_ptiles=rows.n_ptiles, bs=rows.bs)
    k_shift, k_scale = sub * 3, sub * 3 + 1
    return pl.pallas_call(
        kern,
        grid=(rows.n_tiles,),
        in_specs=[
            rows.row_spec(),
            rows.vec_spec(layer * N_SUB + sub),
            rows.modp_spec(layer, k_scale), rows.modp_spec(layer, k_shift),
            rows.mods_spec(layer, k_scale), rows.mods_spec(layer, k_shift),
        ],
        out_specs=rows.row_spec(),
        out_shape=jax.ShapeDtypeStruct((rows.m, rows.d), BF16),
        compiler_params=_params("arbitrary"),
        name="pre_norm",
    )(x, norm_pre, mod_p, mod_p, mod, mod)


def _post(rows, x, o, norm_pre, norm_post, mod_p, mod, layer, sub, res_w, nxt):
    with_next = nxt is not None
    kern = functools.partial(_post_kernel, res_w=res_w, n_ptiles=rows.n_ptiles, bs=rows.bs, with_next=with_next)
    k_gate = sub * 3 + 2
    in_specs = [
        rows.row_spec(), rows.row_spec(),
        rows.vec_spec(layer * N_SUB + sub),
        rows.modp_spec(layer, k_gate), rows.mods_spec(layer, k_gate),
    ]
    args = [x, o, norm_post, mod_p, mod]
    out_specs = [rows.row_spec()]
    out_shape = [jax.ShapeDtypeStruct((rows.m, rows.d), F32)]
    if with_next:
        nl, ns = nxt
        in_specs += [
            rows.vec_spec(nl * N_SUB + ns),
            rows.modp_spec(nl, ns * 3 + 1), rows.modp_spec(nl, ns * 3),
            rows.mods_spec(nl, ns * 3 + 1), rows.mods_spec(nl, ns * 3),
        ]
        args += [norm_pre, mod_p, mod_p, mod, mod]
        out_specs.append(rows.row_spec())
        out_shape.append(jax.ShapeDtypeStruct((rows.m, rows.d), BF16))
    res = pl.pallas_call(
        kern,
        grid=(rows.n_tiles,),
        in_specs=in_specs,
        out_specs=out_specs,
        out_shape=out_shape,
        compiler_params=_params("arbitrary"),
        name="post_residual",
    )(*args)
    return (res[0], res[1]) if with_next else (res[0], None)


def _mm_acc_w32_kernel(x_ref, w_ref, o_ref, *, nk, k_last):
    k = pl.program_id(2)
    bk = w_ref.shape[0]

    def prod(masked):
        w = w_ref[...]
        if masked:
            row = lax.broadcasted_iota(jnp.int32, w.shape, 0)
            w = jnp.where(row < k_last, w, 0.0)
        return jnp.dot(x_ref[...], w.astype(BF16), preferred_element_type=F32)

    ragged = k_last != bk
    if nk == 1:
        o_ref[...] = prod(ragged)
        return

    @pl.when(k == 0)
    def _():
        o_ref[...] = prod(False)

    @pl.when((k > 0) & (k < nk - 1))
    def _():
        o_ref[...] += prod(False)

    @pl.when(k == nk - 1)
    def _():
        o_ref[...] += prod(ragged)


def _mm_acc_w32(x, w_stack, lead):
    m, kp = x.shape
    kd, n = w_stack.shape[-2:]
    bm = _row_block(m)
    bn = _largest_divisor(n, 2048, LANE)
    bk = _largest_divisor(kp, 1024, LANE)
    nk = kp // bk
    assert (nk - 1) * bk < kd <= kp
    none = (None,) * len(lead)
    return pl.pallas_call(
        functools.partial(_mm_acc_w32_kernel, nk=nk, k_last=kd - (nk - 1) * bk),
        grid=(m // bm, n // bn, nk),
        in_specs=[
            pl.BlockSpec((bm, bk), lambda i, j, k: (i, k)),
            pl.BlockSpec(none + (bk, bn), lambda i, j, k: lead + (k, j)),
        ],
        out_specs=pl.BlockSpec((bm, bn), lambda i, j, k: (i, j), pipeline_mode=pl.Buffered(1)),
        out_shape=jax.ShapeDtypeStruct((m, n), F32),
        compiler_params=_params("arbitrary", "arbitrary", "arbitrary"),
        name="matmul_acc_w32",
    )(x, w_stack)


def _mm_w32_kernel(x_ref, wt_ref, o_ref):
    o_ref[...] = lax.dot_general(x_ref[...], wt_ref[...].astype(BF16), NT_DIMS, preferred_element_type=F32)


def _row_block(m):
    return _largest_divisor(m, 2176, BF16_SUBLANE)


def _mm_w32(x, wt_stack, layer, n_cols, *, bn=512):
    m, kd = x.shape
    bm = _row_block(m)
    bn = _largest_divisor(n_cols, bn, LANE)
    return pl.pallas_call(
        _mm_w32_kernel,
        grid=(m // bm, n_cols // bn),
        in_specs=[
            pl.BlockSpec((bm, kd), lambda i, j: (i, 0), pipeline_mode=pl.Buffered(1)),
            pl.BlockSpec((None, bn, kd), lambda i, j: (layer, j, 0)),
        ],
        out_specs=pl.BlockSpec((bm, bn), lambda i, j: (i, j)),
        out_shape=jax.ShapeDtypeStruct((m, n_cols), F32),
        compiler_params=_params("arbitrary", "arbitrary"),
        name="matmul_w32",
    )(x, wt_stack)


def _glu_kernel(x_ref, wg_ref, wu_ref, o_ref, *, nj_valid):
    j = pl.program_id(1)

    @pl.when(j < nj_valid)
    def _():
        x = x_ref[...]
        g = jnp.dot(x, wg_ref[...].astype(BF16), preferred_element_type=F32)
        u = jnp.dot(x, wu_ref[...].astype(BF16), preferred_element_type=F32)
        o_ref[...] = (_silu(g) * u).astype(BF16)

    @pl.when(j >= nj_valid)
    def _():
        o_ref[...] = jnp.zeros_like(o_ref)


def _glu(x, w_in, layer, i, f, fp):
    m, kd = x.shape
    bm = _row_block(m)
    bn = _largest_divisor(f, 256, LANE)
    assert fp % bn == 0
    nj_valid = f // bn
    last = nj_valid - 1
    return pl.pallas_call(
        functools.partial(_glu_kernel, nj_valid=nj_valid),
        grid=(m // bm, fp // bn),
        in_specs=[
            pl.BlockSpec((bm, kd), lambda r, j: (r, 0), pipeline_mode=pl.Buffered(1)),
            pl.BlockSpec((None, None, kd, bn), lambda r, j: (layer, i, 0, jnp.minimum(j, last))),
            pl.BlockSpec((None, None, kd, bn), lambda r, j: (layer, i, 0, nj_valid + jnp.minimum(j, last))),
        ],
        out_specs=pl.BlockSpec((bm, bn), lambda r, j: (r, j)),
        out_shape=jax.ShapeDtypeStruct((m, fp), BF16),
        compiler_params=_params("arbitrary", "arbitrary"),
        name="ffn_in_glu",
    )(x, w_in, w_in)


def _gelu_ln(v, ln_g, ln_b):
    gv = jax.nn.gelu(v)
    mu = jnp.mean(gv, axis=-1, keepdims=True)
    cen = gv - mu
    var = jnp.mean(cen * cen, axis=-1, keepdims=True)
    return cen * lax.rsqrt(var + EPS) * ln_g + ln_b


def _gmlp_p_kernel(u_ref, v_ref, lng_ref, lnb_ref, ws_ref, bs_ref, o_ref, *, n_heads):
    c = GM_CHUNK
    vn = _gelu_ln(v_ref[...], lng_ref[...], lnb_ref[...]).astype(BF16)
    tril = lax.broadcasted_iota(jnp.int32, (c, c), 0) >= lax.broadcasted_iota(jnp.int32, (c, c), 1)
    for h in range(n_heads):
        cols = slice(h * LANE, (h + 1) * LANE)
        w = jnp.where(tril, ws_ref[h], 0.0).astype(BF16)
        f = jnp.dot(w, vn[:, cols], preferred_element_type=F32) + bs_ref[:, cols]
        o_ref[:, cols] = (jax.nn.gelu(u_ref[:, cols]) * f).astype(BF16)


def _gmlp_prompt(p, tp, d, ln_g, ln_b, ws, bs_exp):
    n_heads = d // LANE
    c = GM_CHUNK
    m = p.shape[0]
    return pl.pallas_call(
        functools.partial(_gmlp_p_kernel, n_heads=n_heads),
        grid=(tp // c,),
        in_specs=[
            pl.BlockSpec((c, d), lambda i: (i, 0)),
            pl.BlockSpec((c, d), lambda i: (i, 1)),
            pl.BlockSpec((1, d), lambda i: (0, 0)),
            pl.BlockSpec((1, d), lambda i: (0, 0)),
            pl.BlockSpec((n_heads, c, c), lambda i: (0, 0, 0)),
            pl.BlockSpec((c, d), lambda i: (0, 0)),
        ],
        out_specs=pl.BlockSpec((c, d), lambda i: (i, 0)),
        out_shape=jax.ShapeDtypeStruct((m, 2 * d), BF16),
        compiler_params=_params("arbitrary"),
        name="gmlp_prompt",
    )(p, p, ln_g, ln_b, ws, bs_exp)


def _gmlp_s_kernel(u_ref, v_ref, lng_ref, lnb_ref, wsx_ref, bsx_ref, ya_ref, vn_ref, vnb, facc, *, t_len):
    t = pl.program_id(0)
    vn = _gelu_ln(v_ref[...], lng_ref[...], lnb_ref[...])
    vn_ref[...] = vn
    vnb[t] = vn.astype(BF16)
    facc[...] = jnp.broadcast_to(bsx_ref[t], facc.shape)
    w_t = wsx_ref[t]
    for s in range(t_len):
        @pl.when(s <= t)
        def _():
            w = w_t[s:s + 1, :].astype(BF16).astype(F32)
            facc[...] += w * vnb[s].astype(F32)
    ya_ref[...] = (jax.nn.gelu(u_ref[...]) * facc[...]).astype(BF16)


def _gmlp_sample(uvz3, d, ln_g, ln_b, wsx, bsx):
    t_len, bs, _ = uvz3.shape
    return pl.pallas_call(
        functools.partial(_gmlp_s_kernel, t_len=t_len),
        grid=(t_len,),
        in_specs=[
            pl.BlockSpec((None, bs, d), lambda t: (t, 0, 0)),
            pl.BlockSpec((None, bs, d), lambda t: (t, 0, 1)),
            pl.BlockSpec((1, d), lambda t: (0, 0)),
            pl.BlockSpec((1, d), lambda t: (0, 0)),
            pl.BlockSpec((t_len, t_len, d), lambda t: (0, 0, 0)),
            pl.BlockSpec((t_len, 1, d), lambda t: (0, 0, 0)),
        ],
        out_specs=[
            pl.BlockSpec((None, bs, d), lambda t: (t, 0, 0)),
            pl.BlockSpec((None, bs, d), lambda t: (t, 0, 0)),
        ],
        out_shape=[
            jax.ShapeDtypeStruct((t_len, bs, d), BF16),
            jax.ShapeDtypeStruct((t_len, bs, d), F32),
        ],
        scratch_shapes=[pltpu.VMEM((t_len, bs, d), BF16), pltpu.VMEM((bs, d), F32)],
        compiler_params=_params("arbitrary"),
        name="gmlp_sample",
    )(uvz3, uvz3, ln_g, ln_b, wsx, bsx)


def _head_expand(gw):
    row = lax.broadcasted_iota(jnp.int32, (LANE, gw), 0)
    col = lax.broadcasted_iota(jnp.int32, (LANE, gw), 1)
    return (lax.shift_right_logical(col, 6) == row).astype(BF16)


def _split3(x):
    x1 = x.astype(BF16)
    r1 = x - x1.astype(F32)
    x2 = r1.astype(BF16)
    x3 = (r1 - x2.astype(F32)).astype(BF16)
    return x1, x2, x3


def _dot_f32_by_01(x, m01):
    r = x.shape[0]
    y = jnp.dot(jnp.concatenate(_split3(x), axis=0), m01, preferred_element_type=F32)
    return y[:r] + y[r:2 * r] + y[2 * r:]


def _dot_01_by_f32(m01, x):
    n = x.shape[1]
    y = jnp.dot(m01, jnp.concatenate(_split3(x), axis=1), preferred_element_type=F32)
    return y[:, :n] + y[:, n:2 * n] + y[:, 2 * n:]


def _gated_group_norm(y, z, ng):
    y = y * _silu(z)
    return (_rms(y) * ng).astype(BF16)


def _ssd_p_kernel(x_ref, b_ref, c_ref, z_ref, dt_ref, cwx_ref, cwb_ref, cwc_ref, cbx_ref, cbb_ref, cbc_ref,
                  dtb_ref, alog_ref, dskip_ref, ng_ref, ymix_ref,
                  y_ref, cx_out, cb_out, cc_out, h_out,
                  xpad, bpad, cpad, state_t, ybuf, *, hpg, n_chunks):
    del ymix_ref
    n = SSD_CHUNK
    tail = CONV_W - 1
    first = SUBLANE - tail
    ci = pl.program_id(2)

    @pl.when(ci == 0)
    def _():
        state_t[...] = jnp.zeros_like(state_t)
        for pad in (xpad, bpad, cpad):
            pad[pl.ds(0, SUBLANE), :] = jnp.zeros((SUBLANE, pad.shape[1]), F32)

    def conv_act(src_ref, pad_ref, cw_ref, cb_ref, out_ref):
        x = src_ref[...]
        pad_ref[pl.ds(SUBLANE, n), :] = x
        acc = cb_ref[...] + cw_ref[0:1, :] * pad_ref[pl.ds(first, n), :]
        for k in range(1, tail):
            acc = acc + cw_ref[k:k + 1, :] * pad_ref[pl.ds(first + k, n), :]
        acc = acc + cw_ref[tail:tail + 1, :] * x
        last = src_ref[pl.ds(n - tail, tail), :]
        pad_ref[pl.ds(first, tail), :] = last
        out_ref[...] = last
        return _silu(acc)

    xs = conv_act(x_ref, xpad, cwx_ref, cbx_ref, cx_out)
    bm = conv_act(b_ref, bpad, cwb_ref, cbb_ref, cb_out).astype(BF16)
    cm = conv_act(c_ref, cpad, cwc_ref, cbc_ref, cc_out).astype(BF16)

    gw = xs.shape[1]
    dt = jax.nn.softplus(dt_ref[...] + dtb_ref[...])
    a = dt * (-jnp.exp(alog_ref[...]))
    tril = lax.broadcasted_iota(jnp.int32, (n, n), 0) >= lax.broadcasted_iota(jnp.int32, (n, n), 1)
    a_cum = _dot_01_by_f32(tril.astype(BF16), a)
    a_cum_t = a_cum.T
    both_e = _dot_f32_by_01(jnp.concatenate([dt, a_cum], axis=0), _head_expand(gw))
    dt_e, acum_e = both_e[:n], both_e[n:]
    dtx = dt_e * xs
    w_end = (jnp.exp(acum_e[n - 1:n, :] - acum_e) * dtx).astype(BF16)
    dtx_b = dtx.astype(BF16)
    exp_acum = jnp.exp(acum_e)
    cb = lax.dot_general(cm, bm, NT_DIMS, preferred_element_type=F32)
    h_in = state_t[...]
    y_inter = jnp.dot(cm, h_in.astype(BF16), preferred_element_type=F32) * exp_acum
    new = lax.dot_general(bm, w_end, TN_DIMS, preferred_element_type=F32)
    state_t[...] = h_in * exp_acum[n - 1:n, :] + new
    for j in range(hpg):
        cols = slice(j * HEAD_P, (j + 1) * HEAD_P)
        seg = a_cum[:, j:j + 1] - a_cum_t[j:j + 1, :]
        decay = jnp.exp(jnp.where(tril, seg, -jnp.inf))
        ybuf[:, cols] = jnp.dot((cb * decay).astype(BF16), dtx_b[:, cols], preferred_element_type=F32)
    y = ybuf[...] + y_inter + dskip_ref[...] * xs
    y_ref[...] = _gated_group_norm(y, z_ref[...], ng_ref[...])

    @pl.when(ci == n_chunks - 1)
    def _():
        final = state_t[...].T
        for j in range(hpg):
            h_out[j] = final[j * HEAD_P:(j + 1) * HEAD_P, :]


def _ssd_prompt(p, dtp, y_mix, bp, seq, d, cw8, cb1, dtb_g, alog_g, dskip_e, ng):
    g = N_GROUPS
    gw = d // g
    hpg = gw // HEAD_P
    n = SSD_CHUNK
    nc = seq // n
    tail = CONV_W - 1
    b_off = d // D_STATE
    c_off = b_off + g
    z_off = 2 * d // gw
    x_off = 3 * d // gw
    pb_off = 4 * d // D_STATE
    pc_off = pb_off + g

    def rows(width, off):
        return pl.BlockSpec((n, width), lambda b, gi, c: (b * nc + c, off + gi))

    def par(r, width, off):
        return pl.BlockSpec((r, width), lambda b, gi, c: (0, off + gi))

    def grp():
        return pl.BlockSpec((None, 1, LANE), lambda b, gi, c: (gi, 0, 0))

    def conv_out(width):
        return pl.BlockSpec((None, tail, width), lambda b, gi, c: (b, 0, gi))

    outs = pl.pallas_call(
        functools.partial(_ssd_p_kernel, hpg=hpg, n_chunks=nc),
        grid=(bp, g, nc),
        in_specs=[
            rows(gw, x_off), rows(D_STATE, pb_off), rows(D_STATE, pc_off), rows(gw, z_off), rows(LANE, 0),
            par(SUBLANE, gw, 0), par(SUBLANE, D_STATE, b_off), par(SUBLANE, D_STATE, c_off),
            par(1, gw, 0), par(1, D_STATE, b_off), par(1, D_STATE, c_off),
            grp(), grp(), par(1, gw, 0), par(1, gw, 0),
            pl.BlockSpec(memory_space=pl.ANY),
        ],
        out_specs=[
            pl.BlockSpec((n, gw), lambda b, gi, c: (b * nc + c, g + gi)),
            conv_out(gw), conv_out(D_STATE), conv_out(D_STATE),
            pl.BlockSpec((None, hpg, HEAD_P, D_STATE), lambda b, gi, c: (b, gi, 0, 0)),
        ],
        input_output_aliases={15: 0},
        out_shape=[
            jax.ShapeDtypeStruct(y_mix.shape, BF16),
            jax.ShapeDtypeStruct((bp, tail, d), F32),
            jax.ShapeDtypeStruct((bp, tail, g * D_STATE), F32),
            jax.ShapeDtypeStruct((bp, tail, g * D_STATE), F32),
            jax.ShapeDtypeStruct((bp, g * hpg, HEAD_P, D_STATE), F32),
        ],
        scratch_shapes=[
            pltpu.VMEM((SUBLANE + n, gw), F32),
            pltpu.VMEM((SUBLANE + n, D_STATE), F32),
            pltpu.VMEM((SUBLANE + n, D_STATE), F32),
            pltpu.VMEM((D_STATE, gw), F32),
            pltpu.VMEM((n, gw), F32),
        ],
        compiler_params=_params("arbitrary", "arbitrary", "arbitrary"),
        name="ssd_prompt",
    )(p, p, p, p, dtp, cw8, cw8, cw8, cb1, cb1, cb1, dtb_g, alog_g, dskip_e, ng, y_mix)
    y_mix, cx, cb_, cc, h_t = outs
    return y_mix, jnp.concatenate([cx, cb_, cc], axis=-1), h_t


def _ssd_s_kernel(x_ref, b_ref, c_ref, z_ref, dt_ref, sx_ref, sb_ref, sc_ref,
                  cwx_ref, cwb_ref, cwc_ref, cbx_ref, cbb_ref, cbc_ref,
                  dtb_ref, alog_ref, dskip_ref, ng_ref, h0_ref, *rest, t_len, sb, chained):
    y_ref, cx_out, cb_out, cc_out, h_out, cbuf, bbuf, wbuf, yibuf = rest[1:] if chained else rest
    tail = CONV_W - 1

    def conv_act(src_ref, st_ref, cw_ref, cb_ref, out_ref):
        full = [st_ref[r] for r in range(tail)] + [src_ref[t] for t in range(t_len)]
        acts = []
        for t in range(t_len):
            acc = cb_ref[...] + cw_ref[0:1, :] * full[t]
            for k in range(1, CONV_W):
                acc = acc + cw_ref[k:k + 1, :] * full[t + k]
            acts.append(_silu(acc))
        for r in range(tail):
            out_ref[r] = full[t_len + r]
        return acts

    xs = conv_act(x_ref, sx_ref, cwx_ref, cbx_ref, cx_out)
    bm = conv_act(b_ref, sb_ref, cwb_ref, cbb_ref, cb_out)
    cm = conv_act(c_ref, sc_ref, cwc_ref, cbc_ref, cc_out)
    gw = xs[0].shape[1]

    a_neg = -jnp.exp(alog_ref[...])
    dts, acums = [], []
    run = None
    for t in range(t_len):
        dt = jax.nn.softplus(dt_ref[t] + dtb_ref[...])
        run = dt * a_neg if run is None else run + dt * a_neg
        dts.append(dt)
        acums.append(run)
    both_e = _dot_f32_by_01(jnp.concatenate(dts + acums, axis=0), _head_expand(gw))
    acum_e = [both_e[(t_len + t) * sb:(t_len + t + 1) * sb] for t in range(t_len)]
    dtx = [both_e[t * sb:(t + 1) * sb] * xs[t] for t in range(t_len)]
    a_last = acum_e[t_len - 1]

    n_panels = gw // LANE

    def put_rows(buf, row0, val):
        for p in range(n_panels):
            buf[p, pl.ds(row0, sb), :] = val[:, p * LANE:(p + 1) * LANE]

    def seq_rows(buf, q):
        return jnp.concatenate([buf[p, pl.ds(q, SUBLANE, stride=sb), :] for p in range(n_panels)], axis=1)

    zero_rows = (SUBLANE - t_len) * sb
    for buf in (cbuf, bbuf):
        buf[pl.ds(t_len * sb, zero_rows), :] = jnp.zeros((zero_rows, D_STATE), F32)
    wbuf[:, pl.ds(t_len * sb, zero_rows), :] = jnp.zeros((n_panels, zero_rows, LANE), F32)
    for t in range(t_len):
        cbuf[pl.ds(t * sb, sb), :] = cm[t]
        bbuf[pl.ds(t * sb, sb), :] = bm[t]
        put_rows(wbuf, t * sb, jnp.exp(a_last - acum_e[t]) * dtx[t])
    chunk_decay = jnp.exp(a_last)
    ones = jnp.ones((SUBLANE, D_STATE), BF16)
    for q in range(sb):
        c_q = cbuf[pl.ds(q, SUBLANE, stride=sb), :].astype(BF16)
        b_q = bbuf[pl.ds(q, SUBLANE, stride=sb), :].astype(BF16)
        w_q = seq_rows(wbuf, q).astype(BF16)
        h0 = h0_ref[q]
        y_q = lax.dot_general(c_q, h0.astype(BF16), NT_DIMS, preferred_element_type=F32)
        for p in range(n_panels):
            yibuf[p, pl.ds(q, SUBLANE, stride=sb), :] = y_q[:, p * LANE:(p + 1) * LANE]
        new = lax.dot_general(w_q, b_q, TN_DIMS, preferred_element_type=F32)
        cd = chunk_decay[q:q + 1, :]
        hi = cd.astype(BF16).astype(F32)
        cd8 = jnp.concatenate([hi, cd - hi, jnp.zeros((SUBLANE - 2, gw), F32)], axis=0).astype(BF16)
        cd_col = lax.dot_general(cd8, ones, TN_DIMS, preferred_element_type=F32)
        h_out[q] = h0 * cd_col + new

    for t in range(t_len):
        y = None
        for s in range(t + 1):
            cbts = jnp.sum(cm[t] * bm[s], axis=-1, keepdims=True)
            term = cbts * jnp.exp(acum_e[t] - acum_e[s]) * dtx[s]
            y = term if y is None else y + term
        y_inter = jnp.concatenate([yibuf[p, pl.ds(t * sb, sb), :] for p in range(n_panels)], axis=1)
        y = y + y_inter * jnp.exp(acum_e[t]) + dskip_ref[...] * xs[t]
        y_ref[t] = _gated_group_norm(y, z_ref[t], ng_ref[...])


def _ssd_sample(p3, dt3, conv3, state_ssm4, ssm_prev, layer, d, cw8, cb1, dtb_g, alog_g, dskip_e, ng):
    t_len, bs, _ = p3.shape
    depth = state_ssm4.shape[0]
    assert t_len <= SUBLANE
    g = N_GROUPS
    gw = d // g
    tail = CONV_W - 1
    sb = SUBLANE
    b_off = d // D_STATE
    c_off = b_off + g
    z_off = 2 * d // gw
    x_off = 3 * d // gw
    pb_off = 4 * d // D_STATE
    pc_off = pb_off + g
    chained = ssm_prev is not None

    def slab(r, width, off):
        return pl.BlockSpec((r, sb, width), lambda i, gi: (0, i, off + gi))

    def par(r, width, off):
        return pl.BlockSpec((r, width), lambda i, gi: (0, off + gi))

    def grp():
        return pl.BlockSpec((None, 1, LANE), lambda i, gi: (gi, 0, 0))

    state_spec = pl.BlockSpec((None, sb, gw, D_STATE), lambda i, gi: (layer, i, gi, 0))
    in_specs = [
        slab(t_len, gw, x_off), slab(t_len, D_STATE, pb_off), slab(t_len, D_STATE, pc_off),
        slab(t_len, gw, z_off), slab(t_len, LANE, 0),
        slab(tail, gw, 0), slab(tail, D_STATE, b_off), slab(tail, D_STATE, c_off),
        par(SUBLANE, gw, 0), par(SUBLANE, D_STATE, b_off), par(SUBLANE, D_STATE, c_off),
        par(1, gw, 0), par(1, D_STATE, b_off), par(1, D_STATE, c_off),
        grp(), grp(), par(1, gw, 0), par(1, gw, 0),
        state_spec,
    ]
    args = [p3, p3, p3, p3, dt3, conv3, conv3, conv3, cw8, cw8, cw8, cb1, cb1, cb1,
            dtb_g, alog_g, dskip_e, ng, state_ssm4]
    if chained:
        in_specs.append(pl.BlockSpec(memory_space=pl.ANY))
        args.append(ssm_prev)
    outs = pl.pallas_call(
        functools.partial(_ssd_s_kernel, t_len=t_len, sb=sb, chained=chained),
        grid=(bs // sb, g),
        in_specs=in_specs,
        out_specs=[
            slab(t_len, gw, 0), slab(tail, gw, 0), slab(tail, D_STATE, 0), slab(tail, D_STATE, 0),
            state_spec,
        ],
        input_output_aliases={len(args) - 1: 4} if chained else {},
        out_shape=[
            jax.ShapeDtypeStruct((t_len, bs, d), BF16),
            jax.ShapeDtypeStruct((tail, bs, d), F32),
            jax.ShapeDtypeStruct((tail, bs, g * D_STATE), F32),
            jax.ShapeDtypeStruct((tail, bs, g * D_STATE), F32),
            jax.ShapeDtypeStruct((depth, bs, d, D_STATE), F32),
        ],
        scratch_shapes=[
            pltpu.VMEM((SUBLANE * sb, D_STATE), F32),
            pltpu.VMEM((SUBLANE * sb, D_STATE), F32),
            pltpu.VMEM((gw // LANE, SUBLANE * sb, LANE), F32),
            pltpu.VMEM((gw // LANE, SUBLANE * sb, LANE), F32),
        ],
        compiler_params=_params("arbitrary", "arbitrary"),
        name="ssd_sample",
    )(*args)
    yb, cx, cb_, cc, h_t = outs
    return yb, jnp.concatenate([cx, cb_, cc], axis=-1), h_t


def _group_pad(v):
    hpg = v.shape[0] // N_GROUPS
    return jnp.pad(v.reshape(N_GROUPS, hpg), ((0, 0), (0, LANE - hpg))).reshape(N_GROUPS, 1, LANE)


def kernel(x_prompt, x_sample, state_conv, state_ssm, c_prompt, c_sample, w_ada, b_ada, norm_pre, norm_post,
           ffn_w_in, ffn_w_out, w_in_mix, w_out_mix, gm_ln_g, gm_ln_b, gm_ws, gm_bs, conv_w, conv_b,
           dt_bias, a_log, d_skip, ssm_norm_g):
    bp, seq, d = x_prompt.shape
    bs, t_len, _ = x_sample.shape
    depth = w_ada.shape[0]
    f = ffn_w_out.shape[2]
    h_b = dt_bias.shape[1]
    conv_dim = conv_w.shape[2]
    g = N_GROUPS
    hpg = h_b // g
    tail = CONV_W - 1
    rows = _Rows(bp, seq, bs, t_len, d)
    tp = rows.tp
    fp = -(-f // 1024) * 1024
    n_proj = 3 * d + conv_dim

    x = jnp.concatenate([x_prompt.reshape(tp, d), x_sample.transpose(1, 0, 2).reshape(rows.ts, d)], axis=0)
    mc = -(-(bs + bp) // BF16_SUBLANE) * BF16_SUBLANE
    c_all = jnp.concatenate([c_sample, c_prompt, jnp.zeros((mc - bs - bp, d), F32)], axis=0)
    mod = _ada(c_all, w_ada, b_ada)
    mod_p = mod[:, bs:bs + bp].reshape(depth * bp * 3 * N_SUB, 1, d)
    npre = norm_pre.reshape(depth * N_SUB, 1, d)
    npost = norm_post.reshape(depth * N_SUB, 1, d)
    state_ssm4 = state_ssm.reshape(depth, bs, h_b * HEAD_P, D_STATE)

    def ffn(h, layer, i):
        a = _glu(h, ffn_w_in, layer, i, f, fp)
        return _mm_acc_w32(a, ffn_w_out, (layer, i))

    w_mix_t = jnp.swapaxes(w_in_mix, 1, 2)
    outs = {k: [] for k in ("p_conv", "p_ssm", "s_conv", "s_v")}
    ssm_s = None
    h = _pre(rows, x, npre, mod_p, mod, 0, 0)
    for layer in range(depth):
        x, h = _post(rows, x, ffn(h, layer, 0), npre, npost, mod_p, mod, layer, 0, FFN_RES, (layer, 1))

        p = _mm_w32(h, w_mix_t, layer, n_proj)
        w_dt_t = w_mix_t[layer, n_proj:].reshape(g, hpg, d)
        w_dtp_t = jnp.pad(w_dt_t, ((0, 0), (0, LANE - hpg), (0, 0))).reshape(1, g * LANE, d)
        dtp = _mm_w32(h, w_dtp_t, 0, g * LANE)
        p3 = p[tp:].reshape(t_len, bs, n_proj)
        dt3 = dtp[tp:].reshape(t_len, bs, g * LANE)

        ln_g = gm_ln_g[layer].reshape(1, d)
        ln_b = gm_ln_b[layer].reshape(1, d)
        bs_exp = jnp.repeat(gm_bs[layer].T, LANE, axis=1)
        y_mix = _gmlp_prompt(p, tp, d, ln_g, ln_b, gm_ws[layer], bs_exp)
        wsx = jnp.repeat(gm_ws[layer][:, :t_len, :t_len].transpose(1, 2, 0), LANE, axis=2)
        bsx = jnp.repeat(gm_bs[layer][:, :t_len].T, LANE, axis=1).reshape(t_len, 1, d)
        ya_s, vn_s = _gmlp_sample(p3, d, ln_g, ln_b, wsx, bsx)

        cw8 = jnp.pad(conv_w[layer], ((0, SUBLANE - CONV_W), (0, 0)))
        cb1 = conv_b[layer].reshape(1, conv_dim)
        dtb_g, alog_g = _group_pad(dt_bias[layer]), _group_pad(a_log[layer])
        dskip_e = jnp.repeat(d_skip[layer], HEAD_P).reshape(1, d)
        ng = ssm_norm_g[layer].reshape(1, d)
        y_mix, conv_p, ssm_p = _ssd_prompt(p, dtp, y_mix, bp, seq, d, cw8, cb1, dtb_g, alog_g, dskip_e, ng)
        conv3 = state_conv[layer].transpose(1, 0, 2)
        yb_s, conv_s, ssm_s = _ssd_sample(p3, dt3, conv3, state_ssm4, ssm_s, layer, d,
                                          cw8, cb1, dtb_g, alog_g, dskip_e, ng)
        y_s = jnp.concatenate([ya_s.reshape(rows.ts, d), yb_s.reshape(rows.ts, d)], axis=1)
        y_mix = lax.dynamic_update_slice(y_mix, y_s, (tp, 0))

        o = _mm_acc_w32(y_mix, w_out_mix, (layer,))
        x, h = _post(rows, x, o, npre, npost, mod_p, mod, layer, 1, 1.0, (layer, 2))

        nxt = (layer + 1, 0) if layer + 1 < depth else None
        x, h = _post(rows, x, ffn(h, layer, 1), npre, npost, mod_p, mod, layer, 2, FFN_RES, nxt)

        outs["p_conv"].append(conv_p)
        outs["p_ssm"].append(ssm_p)
        outs["s_conv"].append(conv_s.transpose(1, 0, 2))
        outs["s_v"].append(vn_s.transpose(1, 0, 2))

    y_prompt = x[:tp].reshape(bp, seq, d)
    y_sample = x[tp:].reshape(t_len, bs, d).transpose(1, 0, 2)
    sample_ssm = ssm_s.reshape(depth, bs, h_b, HEAD_P, D_STATE)
    return (y_prompt, y_sample, jnp.stack(outs["p_conv"]), jnp.stack(outs["p_ssm"]),
            jnp.stack(outs["s_conv"]), sample_ssm, jnp.stack(outs["s_v"]))
```

```python
import functools

import jax
import jax.numpy as jnp
from jax import lax
from jax.experimental import pallas as pl
from jax.experimental.pallas import tpu as pltpu

F32 = jnp.float32
BF16 = jnp.bfloat16

GM_CHUNK = 128
SSD_CHUNK = 128
HEAD_P = 64
D_STATE = 128
N_GROUPS = 8
CONV_W = 4
N_SUB = 3
FFN_RES = 0.5
EPS = 1e-6

LANE = 128
SUBLANE = 8
BF16_SUBLANE = 16
VMEM_LIMIT_BYTES = 56 * 2**20
ROW_SUB = 16
ROW_UNROLL = 4
SSD_GROUPS_PER_BLOCK = 8

NT_DIMS = (((1,), (1,)), ((), ()))
TN_DIMS = (((0,), (0,)), ((), ()))


def _params(*sem):
    return pltpu.CompilerParams(dimension_semantics=sem, vmem_limit_bytes=VMEM_LIMIT_BYTES)


def _largest_divisor(n, cap, mult):
    for d in range(min(cap, n), 0, -1):
        if n % d == 0 and d % mult == 0:
            return d
    raise ValueError(f"no tile for {n} (cap {cap}, multiple of {mult})")


def _silu(x):
    return x * (0.5 * jnp.tanh(0.5 * x) + 0.5)


def _rms(x):
    return x * lax.rsqrt(jnp.mean(x * x, axis=-1, keepdims=True) + EPS)


def _ada_kernel(c_ref, w_ref, b_ref, o_ref):
    a = _silu(c_ref[...]).astype(BF16)
    o_ref[...] = jnp.dot(a, w_ref[...].astype(BF16), preferred_element_type=F32) + b_ref[...]


def _ada(c_all, w_ada, b_ada):
    depth, d, n = w_ada.shape
    mc = c_all.shape[0]
    bn = _largest_divisor(n, 512, LANE)
    return pl.pallas_call(
        _ada_kernel,
        grid=(depth, n // bn),
        in_specs=[
            pl.BlockSpec((mc, d), lambda l, j: (0, 0)),
            pl.BlockSpec((None, d, bn), lambda l, j: (l, 0, j)),
            pl.BlockSpec((None, 1, bn), lambda l, j: (l, 0, j)),
        ],
        out_specs=pl.BlockSpec((None, mc, bn), lambda l, j: (l, 0, j)),
        out_shape=jax.ShapeDtypeStruct((depth, mc, n), F32),
        compiler_params=_params("arbitrary", "arbitrary"),
        name="ada_mod",
    )(c_all, w_ada, b_ada.reshape(depth, 1, n))


def _norm_mod(x, g, scale, shift):
    return _rms(x) * g * (1.0 + scale) + shift


def _pre_kernel(xp_ref, xs_ref, g_ref, scp_ref, shp_ref, scs_ref, shs_ref, h_ref, *, n_ptiles, bs):
    i = pl.program_id(0)
    sub = min(ROW_SUB, bs)
    n_sub = h_ref.shape[0] // sub

    @pl.when(i < n_ptiles)
    def _():
        def step(r, carry):
            rows = pl.ds(pl.multiple_of(r * sub, sub), sub)
            h_ref[rows, :] = _norm_mod(xp_ref[rows, :], g_ref[...], scp_ref[...], shp_ref[...]).astype(BF16)
            return carry

        lax.fori_loop(0, n_sub, step, 0, unroll=ROW_UNROLL)

    @pl.when(i >= n_ptiles)
    def _():
        def step(r, carry):
            rows = pl.ds(pl.multiple_of(r * sub, sub), sub)
            seqs = pl.ds(pl.multiple_of(lax.rem(r * sub, bs), sub), sub)
            h_ref[rows, :] = _norm_mod(xs_ref[rows, :], g_ref[...], scs_ref[seqs, :], shs_ref[seqs, :]).astype(BF16)
            return carry

        lax.fori_loop(0, n_sub, step, 0, unroll=ROW_UNROLL)


def _post_kernel(*refs, res_w, n_ptiles, bs, with_next, x_split, out_split):
    refs = list(refs)
    xp_ref = refs.pop(0)
    xs_ref = refs.pop(0) if x_split else xp_ref
    o_ref, gpost_ref, gtp_ref, gts_ref = refs[:4]
    refs = refs[4:]
    if with_next:
        gpre_ref, scp_ref, shp_ref, scs_ref, shs_ref = refs[:5]
        refs = refs[5:]
    xop_ref = refs.pop(0)
    xos_ref = refs.pop(0) if out_split else xop_ref
    h_ref = refs.pop(0) if with_next else None
    i = pl.program_id(0)
    sub = min(ROW_SUB, bs)
    n_sub = o_ref.shape[0] // sub

    def update(x_ref, xo_ref, rows, gate, scale, shift):
        xn = x_ref[rows, :] + res_w * gate * (_rms(o_ref[rows, :]) * gpost_ref[...])
        xo_ref[rows, :] = xn
        if with_next:
            h_ref[rows, :] = _norm_mod(xn, gpre_ref[...], scale, shift).astype(BF16)

    @pl.when(i < n_ptiles)
    def _():
        def step(r, carry):
            rows = pl.ds(pl.multiple_of(r * sub, sub), sub)
            if with_next:
                update(xp_ref, xop_ref, rows, gtp_ref[...], scp_ref[...], shp_ref[...])
            else:
                update(xp_ref, xop_ref, rows, gtp_ref[...], None, None)
            return carry

        lax.fori_loop(0, n_sub, step, 0, unroll=ROW_UNROLL)

    @pl.when(i >= n_ptiles)
    def _():
        def step(r, carry):
            rows = pl.ds(pl.multiple_of(r * sub, sub), sub)
            seqs = pl.ds(pl.multiple_of(lax.rem(r * sub, bs), sub), sub)
            if with_next:
                update(xs_ref, xos_ref, rows, gts_ref[seqs, :], scs_ref[seqs, :], shs_ref[seqs, :])
            else:
                update(xs_ref, xos_ref, rows, gts_ref[seqs, :], None, None)
            return carry

        lax.fori_loop(0, n_sub, step, 0, unroll=ROW_UNROLL)


class _Rows:
    def __init__(self, bp, seq, bs, t, d):
        self.bp, self.seq, self.bs, self.t, self.d = bp, seq, bs, t, d
        self.tp, self.ts = bp * seq, bs * t
        self.m = self.tp + self.ts
        ks = [k for k in range(t, 0, -1) if t % k == 0 and seq % (k * bs) == 0 and (k * bs <= 256 or k == 1)]
        if not ks:
            raise ValueError("prompt length must be a multiple of the sample batch")
        self.bm = ks[0] * bs
        self.n_ptiles = self.tp // self.bm
        self.n_tiles = self.m // self.bm
        self.tiles_per_seq = seq // self.bm

    def row_spec(self):
        return pl.BlockSpec((self.bm, self.d), lambda i: (i, 0))

    def prompt_spec(self):
        last = self.n_ptiles - 1
        return pl.BlockSpec((self.bm, self.d), lambda i: (jnp.minimum(i, last), 0))

    def sample_spec(self):
        first = self.n_ptiles
        return pl.BlockSpec((self.bm, self.d), lambda i: (jnp.maximum(i - first, 0), 0))

    def x_specs(self, x):
        return [self.prompt_spec(), self.sample_spec()] if isinstance(x, tuple) else [self.row_spec()]

    def vec_spec(self, idx):
        return pl.BlockSpec((None, 1, self.d), lambda i: (idx, 0, 0))

    def modp_spec(self, layer, k):
        bp, tps = self.bp, self.tiles_per_seq
        return pl.BlockSpec(
            (None, 1, self.d),
            lambda i: ((layer * bp + jnp.minimum(i // tps, bp - 1)) * (3 * N_SUB) + k, 0, 0))

    def mods_spec(self, layer, k):
        return pl.BlockSpec((None, self.bs, self.d), lambda i: (layer, 0, k))


def _pre(rows, x_pair, norm_pre, mod_p, mod, layer, sub):
    kern = functools.partial(_pre_kernel, n_ptiles=rows.n_ptiles, bs=rows.bs)
    k_shift, k_scale = sub * 3, sub * 3 + 1
    return pl.pallas_call(
        kern,
        grid=(rows.n_tiles,),
        in_specs=[
            rows.prompt_spec(), rows.sample_spec(),
            rows.vec_spec(layer * N_SUB + sub),
            rows.modp_spec(layer, k_scale), rows.modp_spec(layer, k_shift),
            rows.mods_spec(layer, k_scale), rows.mods_spec(layer, k_shift),
        ],
        out_specs=rows.row_spec(),
        out_shape=jax.ShapeDtypeStruct((rows.m, rows.d), BF16),
        compiler_params=_params("arbitrary"),
        name="pre_norm",
    )(*x_pair, norm_pre, mod_p, mod_p, mod, mod)


def _post(rows, x, o, norm_pre, norm_post, mod_p, mod, layer, sub, res_w, nxt):
    with_next = nxt is not None
    x_split = isinstance(x, tuple)
    kern = functools.partial(_post_kernel, res_w=res_w, n_ptiles=rows.n_ptiles, bs=rows.bs, with_next=with_next,
                             x_split=x_split, out_split=not with_next)
    k_gate = sub * 3 + 2
    in_specs = rows.x_specs(x) + [
        rows.row_spec(),
        rows.vec_spec(layer * N_SUB + sub),
        rows.modp_spec(layer, k_gate), rows.mods_spec(layer, k_gate),
    ]
    args = (list(x) if x_split else [x]) + [o, norm_post, mod_p, mod]
    if with_next:
        out_specs = [rows.row_spec()]
        out_shape = [jax.ShapeDtypeStruct((rows.m, rows.d), F32)]
    else:
        out_specs = [rows.prompt_spec(), rows.sample_spec()]
        out_shape = [jax.ShapeDtypeStruct((rows.tp, rows.d), F32), jax.ShapeDtypeStruct((rows.ts, rows.d), F32)]
    if with_next:
        nl, ns = nxt
        in_specs += [
            rows.vec_spec(nl * N_SUB + ns),
            rows.modp_spec(nl, ns * 3 + 1), rows.modp_spec(nl, ns * 3),
            rows.mods_spec(nl, ns * 3 + 1), rows.mods_spec(nl, ns * 3),
        ]
        args += [norm_pre, mod_p, mod_p, mod, mod]
        out_specs.append(rows.row_spec())
        out_shape.append(jax.ShapeDtypeStruct((rows.m, rows.d), BF16))
    res = pl.pallas_call(
        kern,
        grid=(rows.n_tiles,),
        in_specs=in_specs,
        out_specs=out_specs,
        out_shape=out_shape,
        compiler_params=_params("arbitrary"),
        name="post_residual",
    )(*args)
    return (res[0], res[1]) if with_next else ((res[0], res[1]), None)


def _mm_acc_w32_kernel(x_ref, w_ref, o_ref, *, nk, k_last):
    k = pl.program_id(2)
    bk = w_ref.shape[0]

    def prod(masked):
        w = w_ref[...]
        if masked:
            row = lax.broadcasted_iota(jnp.int32, w.shape, 0)
            w = jnp.where(row < k_last, w, 0.0)
        return jnp.dot(x_ref[...], w.astype(BF16), preferred_element_type=F32)

    ragged = k_last != bk
    if nk == 1:
        o_ref[...] = prod(ragged)
        return

    @pl.when(k == 0)
    def _():
        o_ref[...] = prod(False)

    @pl.when((k > 0) & (k < nk - 1))
    def _():
        o_ref[...] += prod(False)

    @pl.when(k == nk - 1)
    def _():
        o_ref[...] += prod(ragged)


def _mm_acc_w32(x, w_stack, lead):
    m, kp = x.shape
    kd, n = w_stack.shape[-2:]
    bm = _row_block(m)
    bn = _largest_divisor(n, 2048, LANE)
    bk = _largest_divisor(kp, 1024, LANE)
    nk = kp // bk
    assert (nk - 1) * bk < kd <= kp
    none = (None,) * len(lead)
    return pl.pallas_call(
        functools.partial(_mm_acc_w32_kernel, nk=nk, k_last=kd - (nk - 1) * bk),
        grid=(m // bm, n // bn, nk),
        in_specs=[
            pl.BlockSpec((bm, bk), lambda i, j, k: (i, k)),
            pl.BlockSpec(none + (bk, bn), lambda i, j, k: lead + (k, j)),
        ],
        out_specs=pl.BlockSpec((bm, bn), lambda i, j, k: (i, j), pipeline_mode=pl.Buffered(1)),
        out_shape=jax.ShapeDtypeStruct((m, n), F32),
        compiler_params=_params("arbitrary", "arbitrary", "arbitrary"),
        name="matmul_acc_w32",
    )(x, w_stack)


def _mm_w32_kernel(x_ref, wt_ref, o_ref):
    o_ref[...] = lax.dot_general(x_ref[...], wt_ref[...].astype(BF16), NT_DIMS, preferred_element_type=F32)


def _row_block(m):
    return _largest_divisor(m, 2176, BF16_SUBLANE)


def _mm_w32(x, wt_stack, layer, n_cols, *, bn=512):
    m, kd = x.shape
    bm = _row_block(m)
    bn = _largest_divisor(n_cols, bn, LANE)
    return pl.pallas_call(
        _mm_w32_kernel,
        grid=(m // bm, n_cols // bn),
        in_specs=[
            pl.BlockSpec((bm, kd), lambda i, j: (i, 0), pipeline_mode=pl.Buffered(1)),
            pl.BlockSpec((None, bn, kd), lambda i, j: (layer, j, 0)),
        ],
        out_specs=pl.BlockSpec((bm, bn), lambda i, j: (i, j)),
        out_shape=jax.ShapeDtypeStruct((m, n_cols), F32),
        compiler_params=_params("arbitrary", "arbitrary"),
        name="matmul_w32",
    )(x, wt_stack)


def _glu_kernel(x_ref, wg_ref, wu_ref, o_ref, *, nj_valid):
    j = pl.program_id(1)

    @pl.when(j < nj_valid)
    def _():
        x = x_ref[...]
        g = jnp.dot(x, wg_ref[...].astype(BF16), preferred_element_type=F32)
        u = jnp.dot(x, wu_ref[...].astype(BF16), preferred_element_type=F32)
        o_ref[...] = (_silu(g) * u).astype(BF16)

    @pl.when(j >= nj_valid)
    def _():
        o_ref[...] = jnp.zeros_like(o_ref)


def _glu(x, w_in, layer, i, f, fp):
    m, kd = x.shape
    bm = _row_block(m)
    bn = _largest_divisor(f, 256, LANE)
    assert fp % bn == 0
    nj_valid = f // bn
    last = nj_valid - 1
    return pl.pallas_call(
        functools.partial(_glu_kernel, nj_valid=nj_valid),
        grid=(m // bm, fp // bn),
        in_specs=[
            pl.BlockSpec((bm, kd), lambda r, j: (r, 0), pipeline_mode=pl.Buffered(1)),
            pl.BlockSpec((None, None, kd, bn), lambda r, j: (layer, i, 0, jnp.minimum(j, last))),
            pl.BlockSpec((None, None, kd, bn), lambda r, j: (layer, i, 0, nj_valid + jnp.minimum(j, last))),
        ],
        out_specs=pl.BlockSpec((bm, bn), lambda r, j: (r, j)),
        out_shape=jax.ShapeDtypeStruct((m, fp), BF16),
        compiler_params=_params("arbitrary", "arbitrary"),
        name="ffn_in_glu",
    )(x, w_in, w_in)


def _gelu_ln(v, ln_g, ln_b):
    gv = jax.nn.gelu(v)
    mu = jnp.mean(gv, axis=-1, keepdims=True)
    cen = gv - mu
    var = jnp.mean(cen * cen, axis=-1, keepdims=True)
    return cen * lax.rsqrt(var + EPS) * ln_g + ln_b


def _gmlp_p_kernel(u_ref, v_ref, lng_ref, lnb_ref, ws_ref, bs_ref, o_ref, *, n_heads):
    c = GM_CHUNK
    vn = _gelu_ln(v_ref[...], lng_ref[...], lnb_ref[...]).astype(BF16)
    tril = lax.broadcasted_iota(jnp.int32, (c, c), 0) >= lax.broadcasted_iota(jnp.int32, (c, c), 1)
    for h in range(n_heads):
        cols = slice(h * LANE, (h + 1) * LANE)
        w = jnp.where(tril, ws_ref[h], 0.0).astype(BF16)
        f = jnp.dot(w, vn[:, cols], preferred_element_type=F32) + bs_ref[:, cols]
        o_ref[:, cols] = (jax.nn.gelu(u_ref[:, cols]) * f).astype(BF16)


def _gmlp_prompt(p, tp, d, ln_g, ln_b, ws, bs_exp):
    n_heads = d // LANE
    c = GM_CHUNK
    m = p.shape[0]
    return pl.pallas_call(
        functools.partial(_gmlp_p_kernel, n_heads=n_heads),
        grid=(tp // c,),
        in_specs=[
            pl.BlockSpec((c, d), lambda i: (i, 0)),
            pl.BlockSpec((c, d), lambda i: (i, 1)),
            pl.BlockSpec((1, d), lambda i: (0, 0)),
            pl.BlockSpec((1, d), lambda i: (0, 0)),
            pl.BlockSpec((n_heads, c, c), lambda i: (0, 0, 0)),
            pl.BlockSpec((c, d), lambda i: (0, 0)),
        ],
        out_specs=pl.BlockSpec((c, d), lambda i: (i, 0)),
        out_shape=jax.ShapeDtypeStruct((m, 2 * d), BF16),
        compiler_params=_params("arbitrary"),
        name="gmlp_prompt",
    )(p, p, ln_g, ln_b, ws, bs_exp)


def _gmlp_s_kernel(u_ref, v_ref, lng_ref, lnb_ref, wsx_ref, bsx_ref, ya_ref, vn_ref, vnb, facc, *, t_len):
    t = pl.program_id(0)
    vn = _gelu_ln(v_ref[...], lng_ref[...], lnb_ref[...])
    vn_ref[...] = vn
    vnb[t] = vn.astype(BF16)
    facc[...] = jnp.broadcast_to(bsx_ref[t], facc.shape)
    w_t = wsx_ref[t]
    for s in range(t_len):
        @pl.when(s <= t)
        def _():
            w = w_t[s:s + 1, :].astype(BF16).astype(F32)
            facc[...] += w * vnb[s].astype(F32)
    ya_ref[...] = (jax.nn.gelu(u_ref[...]) * facc[...]).astype(BF16)


def _gmlp_sample(uvz3, d, ln_g, ln_b, wsx, bsx):
    t_len, bs, _ = uvz3.shape
    return pl.pallas_call(
        functools.partial(_gmlp_s_kernel, t_len=t_len),
        grid=(t_len,),
        in_specs=[
            pl.BlockSpec((None, bs, d), lambda t: (t, 0, 0)),
            pl.BlockSpec((None, bs, d), lambda t: (t, 0, 1)),
            pl.BlockSpec((1, d), lambda t: (0, 0)),
            pl.BlockSpec((1, d), lambda t: (0, 0)),
            pl.BlockSpec((t_len, t_len, d), lambda t: (0, 0, 0)),
            pl.BlockSpec((t_len, 1, d), lambda t: (0, 0, 0)),
        ],
        out_specs=[
            pl.BlockSpec((None, bs, d), lambda t: (t, 0, 0)),
            pl.BlockSpec((None, bs, d), lambda t: (t, 0, 0)),
        ],
        out_shape=[
            jax.ShapeDtypeStruct((t_len, bs, d), BF16),
            jax.ShapeDtypeStruct((t_len, bs, d), F32),
        ],
        scratch_shapes=[pltpu.VMEM((t_len, bs, d), BF16), pltpu.VMEM((bs, d), F32)],
        compiler_params=_params("arbitrary"),
        name="gmlp_sample",
    )(uvz3, uvz3, ln_g, ln_b, wsx, bsx)


def _head_expand(gw):
    row = lax.broadcasted_iota(jnp.int32, (LANE, gw), 0)
    col = lax.broadcasted_iota(jnp.int32, (LANE, gw), 1)
    return (lax.shift_right_logical(col, 6) == row).astype(BF16)


def _split3(x):
    x1 = x.astype(BF16)
    r1 = x - x1.astype(F32)
    x2 = r1.astype(BF16)
    x3 = (r1 - x2.astype(F32)).astype(BF16)
    return x1, x2, x3


def _dot_f32_by_01(x, m01):
    r = x.shape[0]
    y = jnp.dot(jnp.concatenate(_split3(x), axis=0), m01, preferred_element_type=F32)
    return y[:r] + y[r:2 * r] + y[2 * r:]


def _dot_01_by_f32(m01, x):
    n = x.shape[1]
    y = jnp.dot(m01, jnp.concatenate(_split3(x), axis=1), preferred_element_type=F32)
    return y[:, :n] + y[:, n:2 * n] + y[:, 2 * n:]


def _gated_group_norm(y, z, ng):
    y = y * _silu(z)
    return (_rms(y) * ng).astype(BF16)


def _ssd_p_kernel(x_ref, b_ref, c_ref, z_ref, dt_ref, cwx_ref, cwb_ref, cwc_ref, cbx_ref, cbb_ref, cbc_ref,
                  dtb_ref, alog_ref, dskip_ref, ng_ref, ymix_ref,
                  y_ref, cx_out, cb_out, cc_out, h_out,
                  xpad, bpad, cpad, state_t, ybuf, *, hpg, n_chunks, gpb):
    del ymix_ref
    n = SSD_CHUNK
    tail = CONV_W - 1
    first = SUBLANE - tail
    gw = hpg * HEAD_P
    ci = pl.program_id(2)

    @pl.when(ci == 0)
    def _():
        state_t[...] = jnp.zeros_like(state_t)
        for pad in (xpad, bpad, cpad):
            pad[pl.ds(0, SUBLANE), :] = jnp.zeros((SUBLANE, pad.shape[1]), F32)

    def conv_act(src_ref, pad_ref, cw_ref, cb_ref, out_ref, cols):
        x = src_ref[:, cols]
        pad_ref[pl.ds(SUBLANE, n), cols] = x
        acc = cb_ref[:, cols] + cw_ref[0:1, cols] * pad_ref[pl.ds(first, n), cols]
        for k in range(1, tail):
            acc = acc + cw_ref[k:k + 1, cols] * pad_ref[pl.ds(first + k, n), cols]
        acc = acc + cw_ref[tail:tail + 1, cols] * x
        last = src_ref[pl.ds(n - tail, tail), cols]
        pad_ref[pl.ds(first, tail), cols] = last
        out_ref[:, cols] = last
        return _silu(acc)

    tril = lax.broadcasted_iota(jnp.int32, (n, n), 0) >= lax.broadcasted_iota(jnp.int32, (n, n), 1)
    tril_b = tril.astype(BF16)
    expand = _head_expand(gw)
    for s in range(gpb):
        xc = slice(s * gw, (s + 1) * gw)
        sc = slice(s * D_STATE, (s + 1) * D_STATE)
        xs = conv_act(x_ref, xpad, cwx_ref, cbx_ref, cx_out, xc)
        bm = conv_act(b_ref, bpad, cwb_ref, cbb_ref, cb_out, sc).astype(BF16)
        cm = conv_act(c_ref, cpad, cwc_ref, cbc_ref, cc_out, sc).astype(BF16)
        dt = jax.nn.softplus(dt_ref[:, sc] + dtb_ref[s])
        a = dt * (-jnp.exp(alog_ref[s]))
        a_cum = _dot_01_by_f32(tril_b, a)
        a_cum_t = a_cum.T
        both_e = _dot_f32_by_01(jnp.concatenate([dt, a_cum], axis=0), expand)
        dt_e, acum_e = both_e[:n], both_e[n:]
        dtx = dt_e * xs
        w_end = (jnp.exp(acum_e[n - 1:n, :] - acum_e) * dtx).astype(BF16)
        dtx_b = dtx.astype(BF16)
        exp_acum = jnp.exp(acum_e)
        cb = lax.dot_general(cm, bm, NT_DIMS, preferred_element_type=F32)
        h_in = state_t[s]
        y_inter = jnp.dot(cm, h_in.astype(BF16), preferred_element_type=F32) * exp_acum
        new = lax.dot_general(bm, w_end, TN_DIMS, preferred_element_type=F32)
        state_t[s] = h_in * exp_acum[n - 1:n, :] + new
        for j in range(hpg):
            cols = slice(j * HEAD_P, (j + 1) * HEAD_P)
            seg = a_cum[:, j:j + 1] - a_cum_t[j:j + 1, :]
            decay = jnp.exp(jnp.where(tril, seg, -jnp.inf))
            ybuf[:, s * gw + j * HEAD_P:s * gw + (j + 1) * HEAD_P] = jnp.dot(
                (cb * decay).astype(BF16), dtx_b[:, cols], preferred_element_type=F32)
        y = ybuf[:, xc] + y_inter + dskip_ref[:, xc] * xs
        y_ref[:, xc] = _gated_group_norm(y, z_ref[:, xc], ng_ref[:, xc])

    @pl.when(ci == n_chunks - 1)
    def _():
        for s in range(gpb):
            final = state_t[s].T
            for j in range(hpg):
                h_out[s * hpg + j] = final[j * HEAD_P:(j + 1) * HEAD_P, :]


def _ssd_prompt(p, dtp, y_mix, bp, seq, d, cw8, cb1, dtb_g, alog_g, dskip_e, ng):
    g = N_GROUPS
    gpb = SSD_GROUPS_PER_BLOCK
    gw = d // g
    hpg = gw // HEAD_P
    n = SSD_CHUNK
    nc = seq // n
    tail = CONV_W - 1
    xw, sw = gpb * gw, gpb * D_STATE
    b_off = d // sw
    c_off = b_off + g // gpb
    z_off = 2 * d // xw
    x_off = 3 * d // xw
    pb_off = 4 * d // sw
    pc_off = pb_off + g // gpb

    def rows(width, off):
        return pl.BlockSpec((n, width), lambda b, gi, c: (b * nc + c, off + gi))

    def par(r, width, off):
        return pl.BlockSpec((r, width), lambda b, gi, c: (0, off + gi))

    def grp():
        return pl.BlockSpec((gpb, 1, LANE), lambda b, gi, c: (gi, 0, 0))

    def conv_out(width):
        return pl.BlockSpec((None, tail, width), lambda b, gi, c: (b, 0, gi))

    outs = pl.pallas_call(
        functools.partial(_ssd_p_kernel, hpg=hpg, n_chunks=nc, gpb=gpb),
        grid=(bp, g // gpb, nc),
        in_specs=[
            rows(xw, x_off), rows(sw, pb_off), rows(sw, pc_off), rows(xw, z_off), rows(sw, 0),
            par(SUBLANE, xw, 0), par(SUBLANE, sw, b_off), par(SUBLANE, sw, c_off),
            par(1, xw, 0), par(1, sw, b_off), par(1, sw, c_off),
            grp(), grp(), par(1, xw, 0), par(1, xw, 0),
            pl.BlockSpec(memory_space=pl.ANY),
        ],
        out_specs=[
            pl.BlockSpec((n, xw), lambda b, gi, c: (b * nc + c, g // gpb + gi)),
            conv_out(xw), conv_out(sw), conv_out(sw),
            pl.BlockSpec((None, gpb * hpg, HEAD_P, D_STATE), lambda b, gi, c: (b, gi, 0, 0)),
        ],
        input_output_aliases={15: 0},
        out_shape=[
            jax.ShapeDtypeStruct(y_mix.shape, BF16),
            jax.ShapeDtypeStruct((bp, tail, d), F32),
            jax.ShapeDtypeStruct((bp, tail, g * D_STATE), F32),
            jax.ShapeDtypeStruct((bp, tail, g * D_STATE), F32),
            jax.ShapeDtypeStruct((bp, g * hpg, HEAD_P, D_STATE), F32),
        ],
        scratch_shapes=[
            pltpu.VMEM((SUBLANE + n, xw), F32),
            pltpu.VMEM((SUBLANE + n, sw), F32),
            pltpu.VMEM((SUBLANE + n, sw), F32),
            pltpu.VMEM((gpb, D_STATE, gw), F32),
            pltpu.VMEM((n, xw), F32),
        ],
        compiler_params=_params("arbitrary", "arbitrary", "arbitrary"),
        name="ssd_prompt",
    )(p, p, p, p, dtp, cw8, cw8, cw8, cb1, cb1, cb1, dtb_g, alog_g, dskip_e, ng, y_mix)
    y_mix, cx, cb_, cc, h_t = outs
    return y_mix, jnp.concatenate([cx, cb_, cc], axis=-1), h_t


def _ssd_s_kernel(x_ref, b_ref, c_ref, z_ref, dt_ref, sx_ref, sb_ref, sc_ref,
                  cwx_ref, cwb_ref, cwc_ref, cbx_ref, cbb_ref, cbc_ref,
                  dtb_ref, alog_ref, dskip_ref, ng_ref, h0_ref, *rest, t_len, sb, chained):
    y_ref, cx_out, cb_out, cc_out, h_out, cbuf, bbuf, wbuf, yibuf = rest[1:] if chained else rest
    tail = CONV_W - 1

    def conv_act(src_ref, st_ref, cw_ref, cb_ref, out_ref):
        full = [st_ref[r] for r in range(tail)] + [src_ref[t] for t in range(t_len)]
        acts = []
        for t in range(t_len):
            acc = cb_ref[...] + cw_ref[0:1, :] * full[t]
            for k in range(1, CONV_W):
                acc = acc + cw_ref[k:k + 1, :] * full[t + k]
            acts.append(_silu(acc))
        for r in range(tail):
            out_ref[r] = full[t_len + r]
        return acts

    xs = conv_act(x_ref, sx_ref, cwx_ref, cbx_ref, cx_out)
    bm = conv_act(b_ref, sb_ref, cwb_ref, cbb_ref, cb_out)
    cm = conv_act(c_ref, sc_ref, cwc_ref, cbc_ref, cc_out)
    gw = xs[0].shape[1]

    a_neg = -jnp.exp(alog_ref[...])
    dts, acums = [], []
    run = None
    for t in range(t_len):
        dt = jax.nn.softplus(dt_ref[t] + dtb_ref[...])
        run = dt * a_neg if run is None else run + dt * a_neg
        dts.append(dt)
        acums.append(run)
    both_e = _dot_f32_by_01(jnp.concatenate(dts + acums, axis=0), _head_expand(gw))
    acum_e = [both_e[(t_len + t) * sb:(t_len + t + 1) * sb] for t in range(t_len)]
    dtx = [both_e[t * sb:(t + 1) * sb] * xs[t] for t in range(t_len)]
    a_last = acum_e[t_len - 1]

    n_panels = gw // LANE

    def put_rows(buf, row0, val):
        for p in range(n_panels):
            buf[p, pl.ds(row0, sb), :] = val[:, p * LANE:(p + 1) * LANE]

    def seq_rows(buf, q):
        return jnp.concatenate([buf[p, pl.ds(q, SUBLANE, stride=sb), :] for p in range(n_panels)], axis=1)

    zero_rows = (SUBLANE - t_len) * sb
    for buf in (cbuf, bbuf):
        buf[pl.ds(t_len * sb, zero_rows), :] = jnp.zeros((zero_rows, D_STATE), F32)
    wbuf[:, pl.ds(t_len * sb, zero_rows), :] = jnp.zeros((n_panels, zero_rows, LANE), F32)
    for t in range(t_len):
        cbuf[pl.ds(t * sb, sb), :] = cm[t]
        bbuf[pl.ds(t * sb, sb), :] = bm[t]
        put_rows(wbuf, t * sb, jnp.exp(a_last - acum_e[t]) * dtx[t])
    chunk_decay = jnp.exp(a_last)
    ones = jnp.ones((SUBLANE, D_STATE), BF16)
    for q in range(sb):
        c_q = cbuf[pl.ds(q, SUBLANE, stride=sb), :].astype(BF16)
        b_q = bbuf[pl.ds(q, SUBLANE, stride=sb), :].astype(BF16)
        w_q = seq_rows(wbuf, q).astype(BF16)
        h0 = h0_ref[q]
        y_q = lax.dot_general(c_q, h0.astype(BF16), NT_DIMS, preferred_element_type=F32)
        for p in range(n_panels):
            yibuf[p, pl.ds(q, SUBLANE, stride=sb), :] = y_q[:, p * LANE:(p + 1) * LANE]
        new = lax.dot_general(w_q, b_q, TN_DIMS, preferred_element_type=F32)
        cd = chunk_decay[q:q + 1, :]
        hi = cd.astype(BF16).astype(F32)
        cd8 = jnp.concatenate([hi, cd - hi, jnp.zeros((SUBLANE - 2, gw), F32)], axis=0).astype(BF16)
        cd_col = lax.dot_general(cd8, ones, TN_DIMS, preferred_element_type=F32)
        h_out[q] = h0 * cd_col + new

    for t in range(t_len):
        y = None
        for s in range(t + 1):
            cbts = jnp.sum(cm[t] * bm[s], axis=-1, keepdims=True)
            term = cbts * jnp.exp(acum_e[t] - acum_e[s]) * dtx[s]
            y = term if y is None else y + term
        y_inter = jnp.concatenate([yibuf[p, pl.ds(t * sb, sb), :] for p in range(n_panels)], axis=1)
        y = y + y_inter * jnp.exp(acum_e[t]) + dskip_ref[...] * xs[t]
        y_ref[t] = _gated_group_norm(y, z_ref[t], ng_ref[...])


def _ssd_sample(p3, dt3, conv3, state_ssm4, ssm_prev, layer, d, cw8, cb1, dtb_g, alog_g, dskip_e, ng):
    t_len, bs, _ = p3.shape
    depth = state_ssm4.shape[0]
    assert t_len <= SUBLANE
    g = N_GROUPS
    gw = d // g
    tail = CONV_W - 1
    sb = SUBLANE
    b_off = d // D_STATE
    c_off = b_off + g
    z_off = 2 * d // gw
    x_off = 3 * d // gw
    pb_off = 4 * d // D_STATE
    pc_off = pb_off + g
    chained = ssm_prev is not None

    def slab(r, width, off):
        return pl.BlockSpec((r, sb, width), lambda i, gi: (0, i, off + gi))

    def par(r, width, off):
        return pl.BlockSpec((r, width), lambda i, gi: (0, off + gi))

    def grp():
        return pl.BlockSpec((None, 1, LANE), lambda i, gi: (gi, 0, 0))

    state_spec = pl.BlockSpec((None, sb, gw, D_STATE), lambda i, gi: (layer, i, gi, 0))
    in_specs = [
        slab(t_len, gw, x_off), slab(t_len, D_STATE, pb_off), slab(t_len, D_STATE, pc_off),
        slab(t_len, gw, z_off), slab(t_len, LANE, 0),
        slab(tail, gw, 0), slab(tail, D_STATE, b_off), slab(tail, D_STATE, c_off),
        par(SUBLANE, gw, 0), par(SUBLANE, D_STATE, b_off), par(SUBLANE, D_STATE, c_off),
        par(1, gw, 0), par(1, D_STATE, b_off), par(1, D_STATE, c_off),
        grp(), grp(), par(1, gw, 0), par(1, gw, 0),
        state_spec,
    ]
    args = [p3, p3, p3, p3, dt3, conv3, conv3, conv3, cw8, cw8, cw8, cb1, cb1, cb1,
            dtb_g, alog_g, dskip_e, ng, state_ssm4]
    if chained:
        in_specs.append(pl.BlockSpec(memory_space=pl.ANY))
        args.append(ssm_prev)
    outs = pl.pallas_call(
        functools.partial(_ssd_s_kernel, t_len=t_len, sb=sb, chained=chained),
        grid=(bs // sb, g),
        in_specs=in_specs,
        out_specs=[
            slab(t_len, gw, 0), slab(tail, gw, 0), slab(tail, D_STATE, 0), slab(tail, D_STATE, 0),
            state_spec,
        ],
        input_output_aliases={len(args) - 1: 4} if chained else {},
        out_shape=[
            jax.ShapeDtypeStruct((t_len, bs, d), BF16),
            jax.ShapeDtypeStruct((tail, bs, d), F32),
            jax.ShapeDtypeStruct((tail, bs, g * D_STATE), F32),
            jax.ShapeDtypeStruct((tail, bs, g * D_STATE), F32),
            jax.ShapeDtypeStruct((depth, bs, d, D_STATE), F32),
        ],
        scratch_shapes=[
            pltpu.VMEM((SUBLANE * sb, D_STATE), F32),
            pltpu.VMEM((SUBLANE * sb, D_STATE), F32),
            pltpu.VMEM((gw // LANE, SUBLANE * sb, LANE), F32),
            pltpu.VMEM((gw // LANE, SUBLANE * sb, LANE), F32),
        ],
        compiler_params=_params("arbitrary", "arbitrary"),
        name="ssd_sample",
    )(*args)
    yb, cx, cb_, cc, h_t = outs
    return yb, jnp.concatenate([cx, cb_, cc], axis=-1), h_t


def _group_pad(v):
    hpg = v.shape[0] // N_GROUPS
    return jnp.pad(v.reshape(N_GROUPS, hpg), ((0, 0), (0, LANE - hpg))).reshape(N_GROUPS, 1, LANE)


def kernel(x_prompt, x_sample, state_conv, state_ssm, c_prompt, c_sample, w_ada, b_ada, norm_pre, norm_post,
           ffn_w_in, ffn_w_out, w_in_mix, w_out_mix, gm_ln_g, gm_ln_b, gm_ws, gm_bs, conv_w, conv_b,
           dt_bias, a_log, d_skip, ssm_norm_g):
    bp, seq, d = x_prompt.shape
    bs, t_len, _ = x_sample.shape
    depth = w_ada.shape[0]
    f = ffn_w_out.shape[2]
    h_b = dt_bias.shape[1]
    conv_dim = conv_w.shape[2]
    g = N_GROUPS
    hpg = h_b // g
    tail = CONV_W - 1
    rows = _Rows(bp, seq, bs, t_len, d)
    tp = rows.tp
    fp = -(-f // 1024) * 1024
    n_proj = 3 * d + conv_dim

    x = (x_prompt.reshape(tp, d), x_sample.transpose(1, 0, 2).reshape(rows.ts, d))
    mc = -(-(bs + bp) // BF16_SUBLANE) * BF16_SUBLANE
    c_all = jnp.concatenate([c_sample, c_prompt, jnp.zeros((mc - bs - bp, d), F32)], axis=0)
    mod = _ada(c_all, w_ada, b_ada)
    mod_p = mod[:, bs:bs + bp].reshape(depth * bp * 3 * N_SUB, 1, d)
    npre = norm_pre.reshape(depth * N_SUB, 1, d)
    npost = norm_post.reshape(depth * N_SUB, 1, d)
    state_ssm4 = state_ssm.reshape(depth, bs, h_b * HEAD_P, D_STATE)

    def ffn(h, layer, i):
        a = _glu(h, ffn_w_in, layer, i, f, fp)
        return _mm_acc_w32(a, ffn_w_out, (layer, i))

    w_mix_t = jnp.swapaxes(w_in_mix, 1, 2)
    outs = {k: [] for k in ("p_conv", "p_ssm", "s_conv", "s_v")}
    ssm_s = None
    h = _pre(rows, x, npre, mod_p, mod, 0, 0)
    for layer in range(depth):
        x, h = _post(rows, x, ffn(h, layer, 0), npre, npost, mod_p, mod, layer, 0, FFN_RES, (layer, 1))

        p = _mm_w32(h, w_mix_t, layer, n_proj)
        w_dt_t = w_mix_t[layer, n_proj:].reshape(g, hpg, d)
        w_dtp_t = jnp.pad(w_dt_t, ((0, 0), (0, LANE - hpg), (0, 0))).reshape(1, g * LANE, d)
        dtp = _mm_w32(h, w_dtp_t, 0, g * LANE)
        p3 = p[tp:].reshape(t_len, bs, n_proj)
        dt3 = dtp[tp:].reshape(t_len, bs, g * LANE)

        ln_g = gm_ln_g[layer].reshape(1, d)
        ln_b = gm_ln_b[layer].reshape(1, d)
        bs_exp = jnp.repeat(gm_bs[layer].T, LANE, axis=1)
        y_mix = _gmlp_prompt(p, tp, d, ln_g, ln_b, gm_ws[layer], bs_exp)
        wsx = jnp.repeat(gm_ws[layer][:, :t_len, :t_len].transpose(1, 2, 0), LANE, axis=2)
        bsx = jnp.repeat(gm_bs[layer][:, :t_len].T, LANE, axis=1).reshape(t_len, 1, d)
        ya_s, vn_s = _gmlp_sample(p3, d, ln_g, ln_b, wsx, bsx)

        cw8 = jnp.pad(conv_w[layer], ((0, SUBLANE - CONV_W), (0, 0)))
        cb1 = conv_b[layer].reshape(1, conv_dim)
        dtb_g, alog_g = _group_pad(dt_bias[layer]), _group_pad(a_log[layer])
        dskip_e = jnp.repeat(d_skip[layer], HEAD_P).reshape(1, d)
        ng = ssm_norm_g[layer].reshape(1, d)
        y_mix, conv_p, ssm_p = _ssd_prompt(p, dtp, y_mix, bp, seq, d, cw8, cb1, dtb_g, alog_g, dskip_e, ng)
        conv3 = state_conv[layer].transpose(1, 0, 2)
        yb_s, conv_s, ssm_s = _ssd_sample(p3, dt3, conv3, state_ssm4, ssm_s, layer, d,
                                          cw8, cb1, dtb_g, alog_g, dskip_e, ng)
        y_s = jnp.concatenate([ya_s.reshape(rows.ts, d), yb_s.reshape(rows.ts, d)], axis=1)
        y_mix = lax.dynamic_update_slice(y_mix, y_s, (tp, 0))

        o = _mm_acc_w32(y_mix, w_out_mix, (layer,))
        x, h = _post(rows, x, o, npre, npost, mod_p, mod, layer, 1, 1.0, (layer, 2))

        nxt = (layer + 1, 0) if layer + 1 < depth else None
        x, h = _post(rows, x, ffn(h, layer, 1), npre, npost, mod_p, mod, layer, 2, FFN_RES, nxt)

        outs["p_conv"].append(conv_p)
        outs["p_ssm"].append(ssm_p)
        outs["s_conv"].append(conv_s.transpose(1, 0, 2))
        outs["s_v"].append(vn_s.transpose(1, 0, 2))

    y_prompt = x[0].reshape(bp, seq, d)
    y_sample = x[1].reshape(t_len, bs, d).transpose(1, 0, 2)
    sample_ssm = ssm_s.reshape(depth, bs, h_b, HEAD_P, D_STATE)
    return (y_prompt, y_sample, jnp.stack(outs["p_conv"]), jnp.stack(outs["p_ssm"]),
            jnp.stack(outs["s_conv"]), sample_ssm, jnp.stack(outs["s_v"]))
```

```python
import functools

import jax
import jax.numpy as jnp
from jax import lax
from jax.experimental import pallas as pl
from jax.experimental.pallas import tpu as pltpu

F32 = jnp.float32
BF16 = jnp.bfloat16

GM_CHUNK = 128
SSD_CHUNK = 128
HEAD_P = 64
D_STATE = 128
N_GROUPS = 8
CONV_W = 4
N_SUB = 3
FFN_RES = 0.5
EPS = 1e-6

LANE = 128
SUBLANE = 8
BF16_SUBLANE = 16
VMEM_LIMIT_BYTES = 56 * 2**20
ROW_SUB = 16
ROW_UNROLL = 4
SSD_GROUPS_PER_BLOCK = 8

NT_DIMS = (((1,), (1,)), ((), ()))
TN_DIMS = (((0,), (0,)), ((), ()))


def _params(*sem):
    return pltpu.CompilerParams(dimension_semantics=sem, vmem_limit_bytes=VMEM_LIMIT_BYTES)


def _largest_divisor(n, cap, mult):
    for d in range(min(cap, n), 0, -1):
        if n % d == 0 and d % mult == 0:
            return d
    raise ValueError(f"no tile for {n} (cap {cap}, multiple of {mult})")


def _silu(x):
    return x * (0.5 * jnp.tanh(0.5 * x) + 0.5)


def _rms(x):
    return x * lax.rsqrt(jnp.mean(x * x, axis=-1, keepdims=True) + EPS)


def _ada_kernel(c_ref, w_ref, b_ref, o_ref):
    a = _silu(c_ref[...]).astype(BF16)
    o_ref[...] = jnp.dot(a, w_ref[...].astype(BF16), preferred_element_type=F32) + b_ref[...]


def _ada(c_all, w_ada, b_ada):
    depth, d, n = w_ada.shape
    mc = c_all.shape[0]
    bn = _largest_divisor(n, 512, LANE)
    return pl.pallas_call(
        _ada_kernel,
        grid=(depth, n // bn),
        in_specs=[
            pl.BlockSpec((mc, d), lambda l, j: (0, 0)),
            pl.BlockSpec((None, d, bn), lambda l, j: (l, 0, j)),
            pl.BlockSpec((None, 1, bn), lambda l, j: (l, 0, j)),
        ],
        out_specs=pl.BlockSpec((None, mc, bn), lambda l, j: (l, 0, j)),
        out_shape=jax.ShapeDtypeStruct((depth, mc, n), F32),
        compiler_params=_params("arbitrary", "arbitrary"),
        name="ada_mod",
    )(c_all, w_ada, b_ada.reshape(depth, 1, n))


def _norm_mod(x, g, scale, shift):
    return _rms(x) * g * (1.0 + scale) + shift


def _pre_kernel(xp_ref, xs_ref, g_ref, scp_ref, shp_ref, scs_ref, shs_ref, h_ref, *, n_ptiles, bs):
    i = pl.program_id(0)
    sub = min(ROW_SUB, bs)
    n_sub = h_ref.shape[0] // sub

    @pl.when(i < n_ptiles)
    def _():
        def step(r, carry):
            rows = pl.ds(pl.multiple_of(r * sub, sub), sub)
            h_ref[rows, :] = _norm_mod(xp_ref[rows, :], g_ref[...], scp_ref[...], shp_ref[...]).astype(BF16)
            return carry

        lax.fori_loop(0, n_sub, step, 0, unroll=ROW_UNROLL)

    @pl.when(i >= n_ptiles)
    def _():
        def step(r, carry):
            rows = pl.ds(pl.multiple_of(r * sub, sub), sub)
            seqs = pl.ds(pl.multiple_of(lax.rem(r * sub, bs), sub), sub)
            h_ref[rows, :] = _norm_mod(xs_ref[rows, :], g_ref[...], scs_ref[seqs, :], shs_ref[seqs, :]).astype(BF16)
            return carry

        lax.fori_loop(0, n_sub, step, 0, unroll=ROW_UNROLL)


def _post_kernel(*refs, res_w, n_ptiles, bs, with_next, x_split, out_split):
    refs = list(refs)
    xp_ref = refs.pop(0)
    xs_ref = refs.pop(0) if x_split else xp_ref
    o_ref, gpost_ref, gtp_ref, gts_ref = refs[:4]
    refs = refs[4:]
    if with_next:
        gpre_ref, scp_ref, shp_ref, scs_ref, shs_ref = refs[:5]
        refs = refs[5:]
    xop_ref = refs.pop(0)
    xos_ref = refs.pop(0) if out_split else xop_ref
    h_ref = refs.pop(0) if with_next else None
    i = pl.program_id(0)
    sub = min(ROW_SUB, bs)
    n_sub = o_ref.shape[0] // sub

    def update(x_ref, xo_ref, rows, gate, scale, shift):
        xn = x_ref[rows, :] + res_w * gate * (_rms(o_ref[rows, :]) * gpost_ref[...])
        xo_ref[rows, :] = xn
        if with_next:
            h_ref[rows, :] = _norm_mod(xn, gpre_ref[...], scale, shift).astype(BF16)

    @pl.when(i < n_ptiles)
    def _():
        def step(r, carry):
            rows = pl.ds(pl.multiple_of(r * sub, sub), sub)
            if with_next:
                update(xp_ref, xop_ref, rows, gtp_ref[...], scp_ref[...], shp_ref[...])
            else:
                update(xp_ref, xop_ref, rows, gtp_ref[...], None, None)
            return carry

        lax.fori_loop(0, n_sub, step, 0, unroll=ROW_UNROLL)

    @pl.when(i >= n_ptiles)
    def _():
        def step(r, carry):
            rows = pl.ds(pl.multiple_of(r * sub, sub), sub)
            seqs = pl.ds(pl.multiple_of(lax.rem(r * sub, bs), sub), sub)
            if with_next:
                update(xs_ref, xos_ref, rows, gts_ref[seqs, :], scs_ref[seqs, :], shs_ref[seqs, :])
            else:
                update(xs_ref, xos_ref, rows, gts_ref[seqs, :], None, None)
            return carry

        lax.fori_loop(0, n_sub, step, 0, unroll=ROW_UNROLL)


class _Rows:
    def __init__(self, bp, seq, bs, t, d):
        self.bp, self.seq, self.bs, self.t, self.d = bp, seq, bs, t, d
        self.tp, self.ts = bp * seq, bs * t
        self.m = self.tp + self.ts
        ks = [k for k in range(t, 0, -1) if t % k == 0 and seq % (k * bs) == 0 and (k * bs <= 256 or k == 1)]
        if not ks:
            raise ValueError("prompt length must be a multiple of the sample batch")
        self.bm = ks[0] * bs
        self.n_ptiles = self.tp // self.bm
        self.n_tiles = self.m // self.bm
        self.tiles_per_seq = seq // self.bm

    def row_spec(self):
        return pl.BlockSpec((self.bm, self.d), lambda i: (i, 0))

    def prompt_spec(self):
        last = self.n_ptiles - 1
        return pl.BlockSpec((self.bm, self.d), lambda i: (jnp.minimum(i, last), 0))

    def sample_spec(self):
        first = self.n_ptiles
        return pl.BlockSpec((self.bm, self.d), lambda i: (jnp.maximum(i - first, 0), 0))

    def x_specs(self, x):
        return [self.prompt_spec(), self.sample_spec()] if isinstance(x, tuple) else [self.row_spec()]

    def vec_spec(self, idx):
        return pl.BlockSpec((None, 1, self.d), lambda i: (idx, 0, 0))

    def modp_spec(self, layer, k):
        bp, tps = self.bp, self.tiles_per_seq
        return pl.BlockSpec(
            (None, 1, self.d),
            lambda i: ((layer * bp + jnp.minimum(i // tps, bp - 1)) * (3 * N_SUB) + k, 0, 0))

    def mods_spec(self, layer, k):
        return pl.BlockSpec((None, self.bs, self.d), lambda i: (layer, 0, k))


def _pre(rows, x_pair, norm_pre, mod_p, mod, layer, sub):
    kern = functools.partial(_pre_kernel, n_ptiles=rows.n_ptiles, bs=rows.bs)
    k_shift, k_scale = sub * 3, sub * 3 + 1
    return pl.pallas_call(
        kern,
        grid=(rows.n_tiles,),
        in_specs=[
            rows.prompt_spec(), rows.sample_spec(),
            rows.vec_spec(layer * N_SUB + sub),
            rows.modp_spec(layer, k_scale), rows.modp_spec(layer, k_shift),
            rows.mods_spec(layer, k_scale), rows.mods_spec(layer, k_shift),
        ],
        out_specs=rows.row_spec(),
        out_shape=jax.ShapeDtypeStruct((rows.m, rows.d), BF16),
        compiler_params=_params("arbitrary"),
        name="pre_norm",
    )(*x_pair, norm_pre, mod_p, mod_p, mod, mod)


def _post(rows, x, o, norm_pre, norm_post, mod_p, mod, layer, sub, res_w, nxt):
    with_next = nxt is not None
    x_split = isinstance(x, tuple)
    kern = functools.partial(_post_kernel, res_w=res_w, n_ptiles=rows.n_ptiles, bs=rows.bs, with_next=with_next,
                             x_split=x_split, out_split=not with_next)
    k_gate = sub * 3 + 2
    in_specs = rows.x_specs(x) + [
        rows.row_spec(),
        rows.vec_spec(layer * N_SUB + sub),
        rows.modp_spec(layer, k_gate), rows.mods_spec(layer, k_gate),
    ]
    args = (list(x) if x_split else [x]) + [o, norm_post, mod_p, mod]
    if with_next:
        out_specs = [rows.row_spec()]
        out_shape = [jax.ShapeDtypeStruct((rows.m, rows.d), F32)]
    else:
        out_specs = [rows.prompt_spec(), rows.sample_spec()]
        out_shape = [jax.ShapeDtypeStruct((rows.tp, rows.d), F32), jax.ShapeDtypeStruct((rows.ts, rows.d), F32)]
    if with_next:
        nl, ns = nxt
        in_specs += [
            rows.vec_spec(nl * N_SUB + ns),
            rows.modp_spec(nl, ns * 3 + 1), rows.modp_spec(nl, ns * 3),
            rows.mods_spec(nl, ns * 3 + 1), rows.mods_spec(nl, ns * 3),
        ]
        args += [norm_pre, mod_p, mod_p, mod, mod]
        out_specs.append(rows.row_spec())
        out_shape.append(jax.ShapeDtypeStruct((rows.m, rows.d), BF16))
    res = pl.pallas_call(
        kern,
        grid=(rows.n_tiles,),
        in_specs=in_specs,
        out_specs=out_specs,
        out_shape=out_shape,
        compiler_params=_params("arbitrary"),
        name="post_residual",
    )(*args)
    return (res[0], res[1]) if with_next else ((res[0], res[1]), None)


def _mm_acc_w32_kernel(x_ref, w_ref, o_ref, *, nk, k_last):
    k = pl.program_id(2)
    bk = w_ref.shape[0]

    def prod(masked):
        w = w_ref[...]
        if masked:
            row = lax.broadcasted_iota(jnp.int32, w.shape, 0)
            w = jnp.where(row < k_last, w, 0.0)
        return jnp.dot(x_ref[...], w.astype(BF16), preferred_element_type=F32)

    ragged = k_last != bk
    if nk == 1:
        o_ref[...] = prod(ragged)
        return

    @pl.when(k == 0)
    def _():
        o_ref[...] = prod(False)

    @pl.when((k > 0) & (k < nk - 1))
    def _():
        o_ref[...] += prod(False)

    @pl.when(k == nk - 1)
    def _():
        o_ref[...] += prod(ragged)


def _mm_acc_w32(x, w_stack, lead):
    m, kp = x.shape
    kd, n = w_stack.shape[-2:]
    bm = _row_block(m)
    bn = _largest_divisor(n, 2048, LANE)
    bk = _largest_divisor(kp, 1024, LANE)
    nk = kp // bk
    assert (nk - 1) * bk < kd <= kp
    none = (None,) * len(lead)
    return pl.pallas_call(
        functools.partial(_mm_acc_w32_kernel, nk=nk, k_last=kd - (nk - 1) * bk),
        grid=(m // bm, n // bn, nk),
        in_specs=[
            pl.BlockSpec((bm, bk), lambda i, j, k: (i, k)),
            pl.BlockSpec(none + (bk, bn), lambda i, j, k: lead + (k, j)),
        ],
        out_specs=pl.BlockSpec((bm, bn), lambda i, j, k: (i, j), pipeline_mode=pl.Buffered(1)),
        out_shape=jax.ShapeDtypeStruct((m, n), F32),
        compiler_params=_params("arbitrary", "arbitrary", "arbitrary"),
        name="matmul_acc_w32",
    )(x, w_stack)


def _mm_w32_kernel(x_ref, wt_ref, o_ref):
    o_ref[...] = lax.dot_general(x_ref[...], wt_ref[...].astype(BF16), NT_DIMS, preferred_element_type=F32)


def _row_block(m):
    return _largest_divisor(m, 2176, BF16_SUBLANE)


def _mm_w32(x, wt_stack, layer, n_cols, *, bn=512):
    m, kd = x.shape
    bm = _row_block(m)
    bn = _largest_divisor(n_cols, bn, LANE)
    return pl.pallas_call(
        _mm_w32_kernel,
        grid=(m // bm, n_cols // bn),
        in_specs=[
            pl.BlockSpec((bm, kd), lambda i, j: (i, 0), pipeline_mode=pl.Buffered(1)),
            pl.BlockSpec((None, bn, kd), lambda i, j: (layer, j, 0)),
        ],
        out_specs=pl.BlockSpec((bm, bn), lambda i, j: (i, j)),
        out_shape=jax.ShapeDtypeStruct((m, n_cols), F32),
        compiler_params=_params("arbitrary", "arbitrary"),
        name="matmul_w32",
    )(x, wt_stack)


def _glu_kernel(x_ref, wg_ref, wu_ref, o_ref, *, nj_valid):
    j = pl.program_id(1)

    @pl.when(j < nj_valid)
    def _():
        x = x_ref[...]
        g = jnp.dot(x, wg_ref[...].astype(BF16), preferred_element_type=F32)
        u = jnp.dot(x, wu_ref[...].astype(BF16), preferred_element_type=F32)
        o_ref[...] = (_silu(g) * u).astype(BF16)

    @pl.when(j >= nj_valid)
    def _():
        o_ref[...] = jnp.zeros_like(o_ref)


def _glu(x, w_in, layer, i, f, fp):
    m, kd = x.shape
    bm = _row_block(m)
    bn = _largest_divisor(f, 256, LANE)
    assert fp % bn == 0
    nj_valid = f // bn
    last = nj_valid - 1
    return pl.pallas_call(
        functools.partial(_glu_kernel, nj_valid=nj_valid),
        grid=(m // bm, fp // bn),
        in_specs=[
            pl.BlockSpec((bm, kd), lambda r, j: (r, 0), pipeline_mode=pl.Buffered(1)),
            pl.BlockSpec((None, None, kd, bn), lambda r, j: (layer, i, 0, jnp.minimum(j, last))),
            pl.BlockSpec((None, None, kd, bn), lambda r, j: (layer, i, 0, nj_valid + jnp.minimum(j, last))),
        ],
        out_specs=pl.BlockSpec((bm, bn), lambda r, j: (r, j)),
        out_shape=jax.ShapeDtypeStruct((m, fp), BF16),
        compiler_params=_params("arbitrary", "arbitrary"),
        name="ffn_in_glu",
    )(x, w_in, w_in)


def _gelu_ln(v, ln_g, ln_b):
    gv = jax.nn.gelu(v)
    mu = jnp.mean(gv, axis=-1, keepdims=True)
    cen = gv - mu
    var = jnp.mean(cen * cen, axis=-1, keepdims=True)
    return cen * lax.rsqrt(var + EPS) * ln_g + ln_b


def _gmlp_p_kernel(u_ref, v_ref, lng_ref, lnb_ref, ws_ref, bs_ref, o_ref, *, n_heads):
    c = GM_CHUNK
    vn = _gelu_ln(v_ref[...], lng_ref[...], lnb_ref[...]).astype(BF16)
    tril = lax.broadcasted_iota(jnp.int32, (c, c), 0) >= lax.broadcasted_iota(jnp.int32, (c, c), 1)
    for h in range(n_heads):
        cols = slice(h * LANE, (h + 1) * LANE)
        w = jnp.where(tril, ws_ref[h], 0.0).astype(BF16)
        f = jnp.dot(w, vn[:, cols], preferred_element_type=F32) + bs_ref[:, cols]
        o_ref[:, cols] = (jax.nn.gelu(u_ref[:, cols]) * f).astype(BF16)


def _gmlp_prompt(p, tp, d, ln_g, ln_b, ws, bs_exp):
    n_heads = d // LANE
    c = GM_CHUNK
    m = p.shape[0]
    return pl.pallas_call(
        functools.partial(_gmlp_p_kernel, n_heads=n_heads),
        grid=(tp // c,),
        in_specs=[
            pl.BlockSpec((c, d), lambda i: (i, 0)),
            pl.BlockSpec((c, d), lambda i: (i, 1)),
            pl.BlockSpec((1, d), lambda i: (0, 0)),
            pl.BlockSpec((1, d), lambda i: (0, 0)),
            pl.BlockSpec((n_heads, c, c), lambda i: (0, 0, 0)),
            pl.BlockSpec((c, d), lambda i: (0, 0)),
        ],
        out_specs=pl.BlockSpec((c, d), lambda i: (i, 0)),
        out_shape=jax.ShapeDtypeStruct((m, 2 * d), BF16),
        compiler_params=_params("arbitrary"),
        name="gmlp_prompt",
    )(p, p, ln_g, ln_b, ws, bs_exp)


def _gmlp_s_kernel(u_ref, v_ref, lng_ref, lnb_ref, wsx_ref, bsx_ref, ya_ref, vn_ref, vnb, facc, *, t_len):
    t = pl.program_id(0)
    vn = _gelu_ln(v_ref[...], lng_ref[...], lnb_ref[...])
    vn_ref[...] = vn
    vnb[t] = vn.astype(BF16)
    facc[...] = jnp.broadcast_to(bsx_ref[t], facc.shape)
    w_t = wsx_ref[t]
    for s in range(t_len):
        @pl.when(s <= t)
        def _():
            w = w_t[s:s + 1, :].astype(BF16).astype(F32)
            facc[...] += w * vnb[s].astype(F32)
    ya_ref[...] = (jax.nn.gelu(u_ref[...]) * facc[...]).astype(BF16)


def _gmlp_sample(uvz3, d, ln_g, ln_b, wsx, bsx):
    t_len, bs, _ = uvz3.shape
    return pl.pallas_call(
        functools.partial(_gmlp_s_kernel, t_len=t_len),
        grid=(t_len,),
        in_specs=[
            pl.BlockSpec((None, bs, d), lambda t: (t, 0, 0)),
            pl.BlockSpec((None, bs, d), lambda t: (t, 0, 1)),
            pl.BlockSpec((1, d), lambda t: (0, 0)),
            pl.BlockSpec((1, d), lambda t: (0, 0)),
            pl.BlockSpec((t_len, t_len, d), lambda t: (0, 0, 0)),
            pl.BlockSpec((t_len, 1, d), lambda t: (0, 0, 0)),
        ],
        out_specs=[
            pl.BlockSpec((None, bs, d), lambda t: (t, 0, 0)),
            pl.BlockSpec((None, bs, d), lambda t: (t, 0, 0)),
        ],
        out_shape=[
            jax.ShapeDtypeStruct((t_len, bs, d), BF16),
            jax.ShapeDtypeStruct((t_len, bs, d), F32),
        ],
        scratch_shapes=[pltpu.VMEM((t_len, bs, d), BF16), pltpu.VMEM((bs, d), F32)],
        compiler_params=_params("arbitrary"),
        name="gmlp_sample",
    )(uvz3, uvz3, ln_g, ln_b, wsx, bsx)


def _head_expand(gw, first_head=0):
    row = lax.broadcasted_iota(jnp.int32, (LANE, gw), 0)
    col = lax.broadcasted_iota(jnp.int32, (LANE, gw), 1)
    return (lax.shift_right_logical(col, 6) + first_head == row).astype(BF16)


def _split3(x):
    x1 = x.astype(BF16)
    r1 = x - x1.astype(F32)
    x2 = r1.astype(BF16)
    x3 = (r1 - x2.astype(F32)).astype(BF16)
    return x1, x2, x3


def _dot_f32_by_01(x, m01):
    r = x.shape[0]
    y = jnp.dot(jnp.concatenate(_split3(x), axis=0), m01, preferred_element_type=F32)
    return y[:r] + y[r:2 * r] + y[2 * r:]


def _dot_01_by_f32(m01, x):
    n = x.shape[1]
    y = jnp.dot(m01, jnp.concatenate(_split3(x), axis=1), preferred_element_type=F32)
    return y[:, :n] + y[:, n:2 * n] + y[:, 2 * n:]


def _gated_group_norm(y, z, ng):
    y = y * _silu(z)
    return (_rms(y) * ng).astype(BF16)


def _ssd_p_kernel(x_ref, b_ref, c_ref, z_ref, dt_ref, cwx_ref, cwb_ref, cwc_ref, cbx_ref, cbb_ref, cbc_ref,
                  dtb_ref, alog_ref, dskip_ref, ng_ref, ymix_ref,
                  y_ref, cx_out, cb_out, cc_out, h_out,
                  xpad, bpad, cpad, state_t, ybuf, *, hpg, n_chunks, gpb):
    del ymix_ref
    n = SSD_CHUNK
    tail = CONV_W - 1
    first = SUBLANE - tail
    gw = hpg * HEAD_P
    ci = pl.program_id(2)

    @pl.when(ci == 0)
    def _():
        state_t[...] = jnp.zeros_like(state_t)
        for pad in (xpad, bpad, cpad):
            pad[pl.ds(0, SUBLANE), :] = jnp.zeros((SUBLANE, pad.shape[1]), F32)

    def conv_act(src_ref, pad_ref, cw_ref, cb_ref, out_ref, cols):
        x = src_ref[:, cols]
        pad_ref[pl.ds(SUBLANE, n), cols] = x
        acc = cb_ref[:, cols] + cw_ref[0:1, cols] * pad_ref[pl.ds(first, n), cols]
        for k in range(1, tail):
            acc = acc + cw_ref[k:k + 1, cols] * pad_ref[pl.ds(first + k, n), cols]
        acc = acc + cw_ref[tail:tail + 1, cols] * x
        last = src_ref[pl.ds(n - tail, tail), cols]
        pad_ref[pl.ds(first, tail), cols] = last
        out_ref[:, cols] = last
        return _silu(acc)

    tril = lax.broadcasted_iota(jnp.int32, (n, n), 0) >= lax.broadcasted_iota(jnp.int32, (n, n), 1)
    dt = jax.nn.softplus(dt_ref[...] + dtb_ref[...])
    a_cum = _dot_01_by_f32(tril.astype(BF16), dt * (-jnp.exp(alog_ref[...])))
    a_cum_t = a_cum.T
    dt_acum = jnp.concatenate([dt, a_cum], axis=0)
    for s in range(gpb):
        xc = slice(s * gw, (s + 1) * gw)
        sc = slice(s * D_STATE, (s + 1) * D_STATE)
        xs = conv_act(x_ref, xpad, cwx_ref, cbx_ref, cx_out, xc)
        bm = conv_act(b_ref, bpad, cwb_ref, cbb_ref, cb_out, sc).astype(BF16)
        cm = conv_act(c_ref, cpad, cwc_ref, cbc_ref, cc_out, sc).astype(BF16)
        both_e = _dot_f32_by_01(dt_acum, _head_expand(gw, s * hpg))
        dt_e, acum_e = both_e[:n], both_e[n:]
        dtx = dt_e * xs
        w_end = (jnp.exp(acum_e[n - 1:n, :] - acum_e) * dtx).astype(BF16)
        dtx_b = dtx.astype(BF16)
        exp_acum = jnp.exp(acum_e)
        cb = lax.dot_general(cm, bm, NT_DIMS, preferred_element_type=F32)
        h_in = state_t[s]
        y_inter = jnp.dot(cm, h_in.astype(BF16), preferred_element_type=F32) * exp_acum
        new = lax.dot_general(bm, w_end, TN_DIMS, preferred_element_type=F32)
        state_t[s] = h_in * exp_acum[n - 1:n, :] + new
        for j in range(hpg):
            cols = slice(j * HEAD_P, (j + 1) * HEAD_P)
            head = s * hpg + j
            seg = a_cum[:, head:head + 1] - a_cum_t[head:head + 1, :]
            decay = jnp.exp(jnp.where(tril, seg, -jnp.inf))
            ybuf[:, s * gw + j * HEAD_P:s * gw + (j + 1) * HEAD_P] = jnp.dot(
                (cb * decay).astype(BF16), dtx_b[:, cols], preferred_element_type=F32)
        y = ybuf[:, xc] + y_inter + dskip_ref[:, xc] * xs
        y_ref[:, xc] = _gated_group_norm(y, z_ref[:, xc], ng_ref[:, xc])

    @pl.when(ci == n_chunks - 1)
    def _():
        for s in range(gpb):
            final = state_t[s].T
            for j in range(hpg):
                h_out[s * hpg + j] = final[j * HEAD_P:(j + 1) * HEAD_P, :]


def _ssd_prompt(p, dt_all, y_mix, bp, seq, d, cw8, cb1, dtb_a, alog_a, dskip_e, ng):
    g = N_GROUPS
    gpb = SSD_GROUPS_PER_BLOCK
    assert gpb == g
    gw = d // g
    hpg = gw // HEAD_P
    n = SSD_CHUNK
    nc = seq // n
    tail = CONV_W - 1
    xw, sw = gpb * gw, gpb * D_STATE
    b_off = d // sw
    c_off = b_off + g // gpb
    z_off = 2 * d // xw
    x_off = 3 * d // xw
    pb_off = 4 * d // sw
    pc_off = pb_off + g // gpb

    def rows(width, off):
        return pl.BlockSpec((n, width), lambda b, gi, c: (b * nc + c, off + gi))

    def par(r, width, off):
        return pl.BlockSpec((r, width), lambda b, gi, c: (0, off + gi))

    def heads():
        return pl.BlockSpec((1, LANE), lambda b, gi, c: (0, 0))

    def conv_out(width):
        return pl.BlockSpec((None, tail, width), lambda b, gi, c: (b, 0, gi))

    outs = pl.pallas_call(
        functools.partial(_ssd_p_kernel, hpg=hpg, n_chunks=nc, gpb=gpb),
        grid=(bp, g // gpb, nc),
        in_specs=[
            rows(xw, x_off), rows(sw, pb_off), rows(sw, pc_off), rows(xw, z_off),
            pl.BlockSpec((n, LANE), lambda b, gi, c: (b * nc + c, 0)),
            par(SUBLANE, xw, 0), par(SUBLANE, sw, b_off), par(SUBLANE, sw, c_off),
            par(1, xw, 0), par(1, sw, b_off), par(1, sw, c_off),
            heads(), heads(), par(1, xw, 0), par(1, xw, 0),
            pl.BlockSpec(memory_space=pl.ANY),
        ],
        out_specs=[
            pl.BlockSpec((n, xw), lambda b, gi, c: (b * nc + c, g // gpb + gi)),
            conv_out(xw), conv_out(sw), conv_out(sw),
            pl.BlockSpec((None, gpb * hpg, HEAD_P, D_STATE), lambda b, gi, c: (b, gi, 0, 0)),
        ],
        input_output_aliases={15: 0},
        out_shape=[
            jax.ShapeDtypeStruct(y_mix.shape, BF16),
            jax.ShapeDtypeStruct((bp, tail, d), F32),
            jax.ShapeDtypeStruct((bp, tail, g * D_STATE), F32),
            jax.ShapeDtypeStruct((bp, tail, g * D_STATE), F32),
            jax.ShapeDtypeStruct((bp, g * hpg, HEAD_P, D_STATE), F32),
        ],
        scratch_shapes=[
            pltpu.VMEM((SUBLANE + n, xw), F32),
            pltpu.VMEM((SUBLANE + n, sw), F32),
            pltpu.VMEM((SUBLANE + n, sw), F32),
            pltpu.VMEM((gpb, D_STATE, gw), F32),
            pltpu.VMEM((n, xw), F32),
        ],
        compiler_params=_params("arbitrary", "arbitrary", "arbitrary"),
        name="ssd_prompt",
    )(p, p, p, p, dt_all, cw8, cw8, cw8, cb1, cb1, cb1, dtb_a, alog_a, dskip_e, ng, y_mix)
    y_mix, cx, cb_, cc, h_t = outs
    return y_mix, jnp.concatenate([cx, cb_, cc], axis=-1), h_t


def _ssd_s_kernel(x_ref, b_ref, c_ref, z_ref, dt_ref, sx_ref, sb_ref, sc_ref,
                  cwx_ref, cwb_ref, cwc_ref, cbx_ref, cbb_ref, cbc_ref,
                  dtb_ref, alog_ref, dskip_ref, ng_ref, h0_ref, *rest, t_len, sb, chained):
    y_ref, cx_out, cb_out, cc_out, h_out, cbuf, bbuf, wbuf, yibuf = rest[1:] if chained else rest
    tail = CONV_W - 1

    def conv_act(src_ref, st_ref, cw_ref, cb_ref, out_ref):
        full = [st_ref[r] for r in range(tail)] + [src_ref[t] for t in range(t_len)]
        acts = []
        for t in range(t_len):
            acc = cb_ref[...] + cw_ref[0:1, :] * full[t]
            for k in range(1, CONV_W):
                acc = acc + cw_ref[k:k + 1, :] * full[t + k]
            acts.append(_silu(acc))
        for r in range(tail):
            out_ref[r] = full[t_len + r]
        return acts

    xs = conv_act(x_ref, sx_ref, cwx_ref, cbx_ref, cx_out)
    bm = conv_act(b_ref, sb_ref, cwb_ref, cbb_ref, cb_out)
    cm = conv_act(c_ref, sc_ref, cwc_ref, cbc_ref, cc_out)
    gw = xs[0].shape[1]

    a_neg = -jnp.exp(alog_ref[...])
    dts, acums = [], []
    run = None
    for t in range(t_len):
        dt = jax.nn.softplus(dt_ref[t] + dtb_ref[...])
        run = dt * a_neg if run is None else run + dt * a_neg
        dts.append(dt)
        acums.append(run)
    both_e = _dot_f32_by_01(jnp.concatenate(dts + acums, axis=0), _head_expand(gw))
    acum_e = [both_e[(t_len + t) * sb:(t_len + t + 1) * sb] for t in range(t_len)]
    dtx = [both_e[t * sb:(t + 1) * sb] * xs[t] for t in range(t_len)]
    a_last = acum_e[t_len - 1]

    n_panels = gw // LANE

    def put_rows(buf, row0, val):
        for p in range(n_panels):
            buf[p, pl.ds(row0, sb), :] = val[:, p * LANE:(p + 1) * LANE]

    def seq_rows(buf, q):
        return jnp.concatenate([buf[p, pl.ds(q, SUBLANE, stride=sb), :] for p in range(n_panels)], axis=1)

    zero_rows = (SUBLANE - t_len) * sb
    for buf in (cbuf, bbuf):
        buf[pl.ds(t_len * sb, zero_rows), :] = jnp.zeros((zero_rows, D_STATE), F32)
    wbuf[:, pl.ds(t_len * sb, zero_rows), :] = jnp.zeros((n_panels, zero_rows, LANE), F32)
    for t in range(t_len):
        cbuf[pl.ds(t * sb, sb), :] = cm[t]
        bbuf[pl.ds(t * sb, sb), :] = bm[t]
        put_rows(wbuf, t * sb, jnp.exp(a_last - acum_e[t]) * dtx[t])
    chunk_decay = jnp.exp(a_last)
    ones = jnp.ones((SUBLANE, D_STATE), BF16)
    for q in range(sb):
        c_q = cbuf[pl.ds(q, SUBLANE, stride=sb), :].astype(BF16)
        b_q = bbuf[pl.ds(q, SUBLANE, stride=sb), :].astype(BF16)
        w_q = seq_rows(wbuf, q).astype(BF16)
        h0 = h0_ref[q]
        y_q = lax.dot_general(c_q, h0.astype(BF16), NT_DIMS, preferred_element_type=F32)
        for p in range(n_panels):
            yibuf[p, pl.ds(q, SUBLANE, stride=sb), :] = y_q[:, p * LANE:(p + 1) * LANE]
        new = lax.dot_general(w_q, b_q, TN_DIMS, preferred_element_type=F32)
        cd = chunk_decay[q:q + 1, :]
        hi = cd.astype(BF16).astype(F32)
        cd8 = jnp.concatenate([hi, cd - hi, jnp.zeros((SUBLANE - 2, gw), F32)], axis=0).astype(BF16)
        cd_col = lax.dot_general(cd8, ones, TN_DIMS, preferred_element_type=F32)
        h_out[q] = h0 * cd_col + new

    for t in range(t_len):
        y = None
        for s in range(t + 1):
            cbts = jnp.sum(cm[t] * bm[s], axis=-1, keepdims=True)
            term = cbts * jnp.exp(acum_e[t] - acum_e[s]) * dtx[s]
            y = term if y is None else y + term
        y_inter = jnp.concatenate([yibuf[p, pl.ds(t * sb, sb), :] for p in range(n_panels)], axis=1)
        y = y + y_inter * jnp.exp(acum_e[t]) + dskip_ref[...] * xs[t]
        y_ref[t] = _gated_group_norm(y, z_ref[t], ng_ref[...])


def _ssd_sample(p3, dt3, conv3, state_ssm4, ssm_prev, layer, d, cw8, cb1, dtb_g, alog_g, dskip_e, ng):
    t_len, bs, _ = p3.shape
    depth = state_ssm4.shape[0]
    assert t_len <= SUBLANE
    g = N_GROUPS
    gw = d // g
    tail = CONV_W - 1
    sb = SUBLANE
    b_off = d // D_STATE
    c_off = b_off + g
    z_off = 2 * d // gw
    x_off = 3 * d // gw
    pb_off = 4 * d // D_STATE
    pc_off = pb_off + g
    chained = ssm_prev is not None

    def slab(r, width, off):
        return pl.BlockSpec((r, sb, width), lambda i, gi: (0, i, off + gi))

    def par(r, width, off):
        return pl.BlockSpec((r, width), lambda i, gi: (0, off + gi))

    def grp():
        return pl.BlockSpec((None, 1, LANE), lambda i, gi: (gi, 0, 0))

    state_spec = pl.BlockSpec((None, sb, gw, D_STATE), lambda i, gi: (layer, i, gi, 0))
    in_specs = [
        slab(t_len, gw, x_off), slab(t_len, D_STATE, pb_off), slab(t_len, D_STATE, pc_off),
        slab(t_len, gw, z_off), slab(t_len, LANE, 0),
        slab(tail, gw, 0), slab(tail, D_STATE, b_off), slab(tail, D_STATE, c_off),
        par(SUBLANE, gw, 0), par(SUBLANE, D_STATE, b_off), par(SUBLANE, D_STATE, c_off),
        par(1, gw, 0), par(1, D_STATE, b_off), par(1, D_STATE, c_off),
        grp(), grp(), par(1, gw, 0), par(1, gw, 0),
        state_spec,
    ]
    args = [p3, p3, p3, p3, dt3, conv3, conv3, conv3, cw8, cw8, cw8, cb1, cb1, cb1,
            dtb_g, alog_g, dskip_e, ng, state_ssm4]
    if chained:
        in_specs.append(pl.BlockSpec(memory_space=pl.ANY))
        args.append(ssm_prev)
    outs = pl.pallas_call(
        functools.partial(_ssd_s_kernel, t_len=t_len, sb=sb, chained=chained),
        grid=(bs // sb, g),
        in_specs=in_specs,
        out_specs=[
            slab(t_len, gw, 0), slab(tail, gw, 0), slab(tail, D_STATE, 0), slab(tail, D_STATE, 0),
            state_spec,
        ],
        input_output_aliases={len(args) - 1: 4} if chained else {},
        out_shape=[
            jax.ShapeDtypeStruct((t_len, bs, d), BF16),
            jax.ShapeDtypeStruct((tail, bs, d), F32),
            jax.ShapeDtypeStruct((tail, bs, g * D_STATE), F32),
            jax.ShapeDtypeStruct((tail, bs, g * D_STATE), F32),
            jax.ShapeDtypeStruct((depth, bs, d, D_STATE), F32),
        ],
        scratch_shapes=[
            pltpu.VMEM((SUBLANE * sb, D_STATE), F32),
            pltpu.VMEM((SUBLANE * sb, D_STATE), F32),
            pltpu.VMEM((gw // LANE, SUBLANE * sb, LANE), F32),
            pltpu.VMEM((gw // LANE, SUBLANE * sb, LANE), F32),
        ],
        compiler_params=_params("arbitrary", "arbitrary"),
        name="ssd_sample",
    )(*args)
    yb, cx, cb_, cc, h_t = outs
    return yb, jnp.concatenate([cx, cb_, cc], axis=-1), h_t


def _group_pad(v):
    hpg = v.shape[0] // N_GROUPS
    return jnp.pad(v.reshape(N_GROUPS, hpg), ((0, 0), (0, LANE - hpg))).reshape(N_GROUPS, 1, LANE)


def kernel(x_prompt, x_sample, state_conv, state_ssm, c_prompt, c_sample, w_ada, b_ada, norm_pre, norm_post,
           ffn_w_in, ffn_w_out, w_in_mix, w_out_mix, gm_ln_g, gm_ln_b, gm_ws, gm_bs, conv_w, conv_b,
           dt_bias, a_log, d_skip, ssm_norm_g):
    bp, seq, d = x_prompt.shape
    bs, t_len, _ = x_sample.shape
    depth = w_ada.shape[0]
    f = ffn_w_out.shape[2]
    h_b = dt_bias.shape[1]
    conv_dim = conv_w.shape[2]
    g = N_GROUPS
    hpg = h_b // g
    tail = CONV_W - 1
    rows = _Rows(bp, seq, bs, t_len, d)
    tp = rows.tp
    fp = -(-f // 1024) * 1024
    n_proj = 3 * d + conv_dim

    x = (x_prompt.reshape(tp, d), x_sample.transpose(1, 0, 2).reshape(rows.ts, d))
    mc = -(-(bs + bp) // BF16_SUBLANE) * BF16_SUBLANE
    c_all = jnp.concatenate([c_sample, c_prompt, jnp.zeros((mc - bs - bp, d), F32)], axis=0)
    mod = _ada(c_all, w_ada, b_ada)
    mod_p = mod[:, bs:bs + bp].reshape(depth * bp * 3 * N_SUB, 1, d)
    npre = norm_pre.reshape(depth * N_SUB, 1, d)
    npost = norm_post.reshape(depth * N_SUB, 1, d)
    state_ssm4 = state_ssm.reshape(depth, bs, h_b * HEAD_P, D_STATE)

    def ffn(h, layer, i):
        a = _glu(h, ffn_w_in, layer, i, f, fp)
        return _mm_acc_w32(a, ffn_w_out, (layer, i))

    w_mix_t = jnp.swapaxes(w_in_mix, 1, 2)
    outs = {k: [] for k in ("p_conv", "p_ssm", "s_conv", "s_v")}
    ssm_s = None
    h = _pre(rows, x, npre, mod_p, mod, 0, 0)
    for layer in range(depth):
        x, h = _post(rows, x, ffn(h, layer, 0), npre, npost, mod_p, mod, layer, 0, FFN_RES, (layer, 1))

        p = _mm_w32(h, w_mix_t, layer, n_proj)
        w_dt_t = w_mix_t[layer, n_proj:]
        dt_all = _mm_w32(h, jnp.pad(w_dt_t, ((0, LANE - h_b), (0, 0))).reshape(1, LANE, d), 0, LANE)
        w_dtp_t = jnp.pad(w_dt_t.reshape(g, hpg, d), ((0, 0), (0, LANE - hpg), (0, 0))).reshape(1, g * LANE, d)
        dt3 = _mm_w32(h[tp:], w_dtp_t, 0, g * LANE).reshape(t_len, bs, g * LANE)
        p3 = p[tp:].reshape(t_len, bs, n_proj)

        ln_g = gm_ln_g[layer].reshape(1, d)
        ln_b = gm_ln_b[layer].reshape(1, d)
        bs_exp = jnp.repeat(gm_bs[layer].T, LANE, axis=1)
        y_mix = _gmlp_prompt(p, tp, d, ln_g, ln_b, gm_ws[layer], bs_exp)
        wsx = jnp.repeat(gm_ws[layer][:, :t_len, :t_len].transpose(1, 2, 0), LANE, axis=2)
        bsx = jnp.repeat(gm_bs[layer][:, :t_len].T, LANE, axis=1).reshape(t_len, 1, d)
        ya_s, vn_s = _gmlp_sample(p3, d, ln_g, ln_b, wsx, bsx)

        cw8 = jnp.pad(conv_w[layer], ((0, SUBLANE - CONV_W), (0, 0)))
        cb1 = conv_b[layer].reshape(1, conv_dim)
        dtb_g, alog_g = _group_pad(dt_bias[layer]), _group_pad(a_log[layer])
        dskip_e = jnp.repeat(d_skip[layer], HEAD_P).reshape(1, d)
        ng = ssm_norm_g[layer].reshape(1, d)
        dtb_a = jnp.pad(dt_bias[layer], (0, LANE - h_b)).reshape(1, LANE)
        alog_a = jnp.pad(a_log[layer], (0, LANE - h_b)).reshape(1, LANE)
        y_mix, conv_p, ssm_p = _ssd_prompt(p, dt_all, y_mix, bp, seq, d, cw8, cb1, dtb_a, alog_a, dskip_e, ng)
        conv3 = state_conv[layer].transpose(1, 0, 2)
        yb_s, conv_s, ssm_s = _ssd_sample(p3, dt3, conv3, state_ssm4, ssm_s, layer, d,
                                          cw8, cb1, dtb_g, alog_g, dskip_e, ng)
        y_s = jnp.concatenate([ya_s.reshape(rows.ts, d), yb_s.reshape(rows.ts, d)], axis=1)
        y_mix = lax.dynamic_update_slice(y_mix, y_s, (tp, 0))

        o = _mm_acc_w32(y_mix, w_out_mix, (layer,))
        x, h = _post(rows, x, o, npre, npost, mod_p, mod, layer, 1, 1.0, (layer, 2))

        nxt = (layer + 1, 0) if layer + 1 < depth else None
        x, h = _post(rows, x, ffn(h, layer, 1), npre, npost, mod_p, mod, layer, 2, FFN_RES, nxt)

        outs["p_conv"].append(conv_p)
        outs["p_ssm"].append(ssm_p)
        outs["s_conv"].append(conv_s.transpose(1, 0, 2))
        outs["s_v"].append(vn_s.transpose(1, 0, 2))

    y_prompt = x[0].reshape(bp, seq, d)
    y_sample = x[1].reshape(t_len, bs, d).transpose(1, 0, 2)
    sample_ssm = ssm_s.reshape(depth, bs, h_b, HEAD_P, D_STATE)
    return (y_prompt, y_sample, jnp.stack(outs["p_conv"]), jnp.stack(outs["p_ssm"]),
            jnp.stack(outs["s_conv"]), sample_ssm, jnp.stack(outs["s_v"]))
```

```python
import functools

import jax
import jax.numpy as jnp
from jax import lax
from jax.experimental import pallas as pl
from jax.experimental.pallas import tpu as pltpu

F32 = jnp.float32
BF16 = jnp.bfloat16

GM_CHUNK = 128
SSD_CHUNK = 128
HEAD_P = 64
D_STATE = 128
N_GROUPS = 8
CONV_W = 4
N_SUB = 3
FFN_RES = 0.5
EPS = 1e-6

LANE = 128
SUBLANE = 8
BF16_SUBLANE = 16
VMEM_LIMIT_BYTES = 56 * 2**20
ROW_SUB = 16
ROW_UNROLL = 4
SSD_GROUPS_PER_BLOCK = 8

NT_DIMS = (((1,), (1,)), ((), ()))
TN_DIMS = (((0,), (0,)), ((), ()))


def _params(*sem):
    return pltpu.CompilerParams(dimension_semantics=sem, vmem_limit_bytes=VMEM_LIMIT_BYTES)


def _largest_divisor(n, cap, mult):
    for d in range(min(cap, n), 0, -1):
        if n % d == 0 and d % mult == 0:
            return d
    raise ValueError(f"no tile for {n} (cap {cap}, multiple of {mult})")


def _silu(x):
    return x * (0.5 * jnp.tanh(0.5 * x) + 0.5)


def _rms(x):
    return x * lax.rsqrt(jnp.mean(x * x, axis=-1, keepdims=True) + EPS)


def _ada_kernel(c_ref, w_ref, b_ref, o_ref):
    a = _silu(c_ref[...]).astype(BF16)
    o_ref[...] = jnp.dot(a, w_ref[...].astype(BF16), preferred_element_type=F32) + b_ref[...]


def _ada(c_all, w_ada, b_ada):
    depth, d, n = w_ada.shape
    mc = c_all.shape[0]
    bn = _largest_divisor(n, 512, LANE)
    return pl.pallas_call(
        _ada_kernel,
        grid=(depth, n // bn),
        in_specs=[
            pl.BlockSpec((mc, d), lambda l, j: (0, 0)),
            pl.BlockSpec((None, d, bn), lambda l, j: (l, 0, j)),
            pl.BlockSpec((None, 1, bn), lambda l, j: (l, 0, j)),
        ],
        out_specs=pl.BlockSpec((None, mc, bn), lambda l, j: (l, 0, j)),
        out_shape=jax.ShapeDtypeStruct((depth, mc, n), F32),
        compiler_params=_params("arbitrary", "arbitrary"),
        name="ada_mod",
    )(c_all, w_ada, b_ada.reshape(depth, 1, n))


def _norm_mod(x, g, scale, shift):
    return _rms(x) * g * (1.0 + scale) + shift


def _pre_kernel(xp_ref, xs_ref, g_ref, scp_ref, shp_ref, scs_ref, shs_ref, h_ref, *, n_ptiles, bs):
    i = pl.program_id(0)
    sub = min(ROW_SUB, bs)
    n_sub = h_ref.shape[0] // sub

    @pl.when(i < n_ptiles)
    def _():
        def step(r, carry):
            rows = pl.ds(pl.multiple_of(r * sub, sub), sub)
            h_ref[rows, :] = _norm_mod(xp_ref[rows, :], g_ref[...], scp_ref[...], shp_ref[...]).astype(BF16)
            return carry

        lax.fori_loop(0, n_sub, step, 0, unroll=ROW_UNROLL)

    @pl.when(i >= n_ptiles)
    def _():
        def step(r, carry):
            rows = pl.ds(pl.multiple_of(r * sub, sub), sub)
            seqs = pl.ds(pl.multiple_of(lax.rem(r * sub, bs), sub), sub)
            h_ref[rows, :] = _norm_mod(xs_ref[rows, :], g_ref[...], scs_ref[seqs, :], shs_ref[seqs, :]).astype(BF16)
            return carry

        lax.fori_loop(0, n_sub, step, 0, unroll=ROW_UNROLL)


def _post_kernel(*refs, res_w, n_ptiles, bs, with_next, x_split, out_split):
    refs = list(refs)
    xp_ref = refs.pop(0)
    xs_ref = refs.pop(0) if x_split else xp_ref
    o_ref, gpost_ref, gtp_ref, gts_ref = refs[:4]
    refs = refs[4:]
    if with_next:
        gpre_ref, scp_ref, shp_ref, scs_ref, shs_ref = refs[:5]
        refs = refs[5:]
    xop_ref = refs.pop(0)
    xos_ref = refs.pop(0) if out_split else xop_ref
    h_ref = refs.pop(0) if with_next else None
    i = pl.program_id(0)
    sub = min(ROW_SUB, bs)
    n_sub = o_ref.shape[0] // sub

    def update(x_ref, xo_ref, rows, gate, scale, shift):
        xn = x_ref[rows, :] + res_w * gate * (_rms(o_ref[rows, :]) * gpost_ref[...])
        xo_ref[rows, :] = xn
        if with_next:
            h_ref[rows, :] = _norm_mod(xn, gpre_ref[...], scale, shift).astype(BF16)

    @pl.when(i < n_ptiles)
    def _():
        def step(r, carry):
            rows = pl.ds(pl.multiple_of(r * sub, sub), sub)
            if with_next:
                update(xp_ref, xop_ref, rows, gtp_ref[...], scp_ref[...], shp_ref[...])
            else:
                update(xp_ref, xop_ref, rows, gtp_ref[...], None, None)
            return carry

        lax.fori_loop(0, n_sub, step, 0, unroll=ROW_UNROLL)

    @pl.when(i >= n_ptiles)
    def _():
        def step(r, carry):
            rows = pl.ds(pl.multiple_of(r * sub, sub), sub)
            seqs = pl.ds(pl.multiple_of(lax.rem(r * sub, bs), sub), sub)
            if with_next:
                update(xs_ref, xos_ref, rows, gts_ref[seqs, :], scs_ref[seqs, :], shs_ref[seqs, :])
            else:
                update(xs_ref, xos_ref, rows, gts_ref[seqs, :], None, None)
            return carry

        lax.fori_loop(0, n_sub, step, 0, unroll=ROW_UNROLL)


class _Rows:
    def __init__(self, bp, seq, bs, t, d):
        self.bp, self.seq, self.bs, self.t, self.d = bp, seq, bs, t, d
        self.tp, self.ts = bp * seq, bs * t
        self.m = self.tp + self.ts
        ks = [k for k in range(t, 0, -1) if t % k == 0 and seq % (k * bs) == 0 and (k * bs <= 256 or k == 1)]
        if not ks:
            raise ValueError("prompt length must be a multiple of the sample batch")
        self.bm = ks[0] * bs
        self.n_ptiles = self.tp // self.bm
        self.n_tiles = self.m // self.bm
        self.tiles_per_seq = seq // self.bm

    def row_spec(self):
        return pl.BlockSpec((self.bm, self.d), lambda i: (i, 0))

    def prompt_spec(self):
        last = self.n_ptiles - 1
        return pl.BlockSpec((self.bm, self.d), lambda i: (jnp.minimum(i, last), 0))

    def sample_spec(self):
        first = self.n_ptiles
        return pl.BlockSpec((self.bm, self.d), lambda i: (jnp.maximum(i - first, 0), 0))

    def x_specs(self, x):
        return [self.prompt_spec(), self.sample_spec()] if isinstance(x, tuple) else [self.row_spec()]

    def vec_spec(self, idx):
        return pl.BlockSpec((None, 1, self.d), lambda i: (idx, 0, 0))

    def modp_spec(self, layer, k):
        bp, tps = self.bp, self.tiles_per_seq
        return pl.BlockSpec(
            (None, 1, self.d),
            lambda i: ((layer * bp + jnp.minimum(i // tps, bp - 1)) * (3 * N_SUB) + k, 0, 0))

    def mods_spec(self, layer, k):
        return pl.BlockSpec((None, self.bs, self.d), lambda i: (layer, 0, k))


def _pre(rows, x_pair, norm_pre, mod_p, mod, layer, sub):
    kern = functools.partial(_pre_kernel, n_ptiles=rows.n_ptiles, bs=rows.bs)
    k_shift, k_scale = sub * 3, sub * 3 + 1
    return pl.pallas_call(
        kern,
        grid=(rows.n_tiles,),
        in_specs=[
            rows.prompt_spec(), rows.sample_spec(),
            rows.vec_spec(layer * N_SUB + sub),
            rows.modp_spec(layer, k_scale), rows.modp_spec(layer, k_shift),
            rows.mods_spec(layer, k_scale), rows.mods_spec(layer, k_shift),
        ],
        out_specs=rows.row_spec(),
        out_shape=jax.ShapeDtypeStruct((rows.m, rows.d), BF16),
        compiler_params=_params("arbitrary"),
        name="pre_norm",
    )(*x_pair, norm_pre, mod_p, mod_p, mod, mod)


def _post(rows, x, o, norm_pre, norm_post, mod_p, mod, layer, sub, res_w, nxt):
    with_next = nxt is not None
    x_split = isinstance(x, tuple)
    kern = functools.partial(_post_kernel, res_w=res_w, n_ptiles=rows.n_ptiles, bs=rows.bs, with_next=with_next,
                             x_split=x_split, out_split=not with_next)
    k_gate = sub * 3 + 2
    in_specs = rows.x_specs(x) + [
        rows.row_spec(),
        rows.vec_spec(layer * N_SUB + sub),
        rows.modp_spec(layer, k_gate), rows.mods_spec(layer, k_gate),
    ]
    args = (list(x) if x_split else [x]) + [o, norm_post, mod_p, mod]
    if with_next:
        out_specs = [rows.row_spec()]
        out_shape = [jax.ShapeDtypeStruct((rows.m, rows.d), F32)]
    else:
        out_specs = [rows.prompt_spec(), rows.sample_spec()]
        out_shape = [jax.ShapeDtypeStruct((rows.tp, rows.d), F32), jax.ShapeDtypeStruct((rows.ts, rows.d), F32)]
    if with_next:
        nl, ns = nxt
        in_specs += [
            rows.vec_spec(nl * N_SUB + ns),
            rows.modp_spec(nl, ns * 3 + 1), rows.modp_spec(nl, ns * 3),
            rows.mods_spec(nl, ns * 3 + 1), rows.mods_spec(nl, ns * 3),
        ]
        args += [norm_pre, mod_p, mod_p, mod, mod]
        out_specs.append(rows.row_spec())
        out_shape.append(jax.ShapeDtypeStruct((rows.m, rows.d), BF16))
    res = pl.pallas_call(
        kern,
        grid=(rows.n_tiles,),
        in_specs=in_specs,
        out_specs=out_specs,
        out_shape=out_shape,
        compiler_params=_params("arbitrary"),
        name="post_residual",
    )(*args)
    return (res[0], res[1]) if with_next else ((res[0], res[1]), None)


def _mm_acc_w32_kernel(x_ref, w_ref, o_ref, *, nk, k_last):
    k = pl.program_id(2)
    bk = w_ref.shape[0]

    def prod(masked):
        if masked and k_last % LANE == 0:
            return jnp.dot(x_ref[:, :k_last], w_ref[:k_last, :].astype(BF16), preferred_element_type=F32)
        w = w_ref[...]
        if masked:
            row = lax.broadcasted_iota(jnp.int32, w.shape, 0)
            w = jnp.where(row < k_last, w, 0.0)
        return jnp.dot(x_ref[...], w.astype(BF16), preferred_element_type=F32)

    ragged = k_last != bk
    if nk == 1:
        o_ref[...] = prod(ragged)
        return

    @pl.when(k == 0)
    def _():
        o_ref[...] = prod(False)

    @pl.when((k > 0) & (k < nk - 1))
    def _():
        o_ref[...] += prod(False)

    @pl.when(k == nk - 1)
    def _():
        o_ref[...] += prod(ragged)


def _mm_acc_w32(x, w_stack, lead):
    m, kp = x.shape
    kd, n = w_stack.shape[-2:]
    bm = _row_block(m)
    bn = _largest_divisor(n, 2048, LANE)
    bk = _largest_divisor(kp, 1024, LANE)
    nk = kp // bk
    assert (nk - 1) * bk < kd <= kp
    none = (None,) * len(lead)
    return pl.pallas_call(
        functools.partial(_mm_acc_w32_kernel, nk=nk, k_last=kd - (nk - 1) * bk),
        grid=(m // bm, n // bn, nk),
        in_specs=[
            pl.BlockSpec((bm, bk), lambda i, j, k: (i, k)),
            pl.BlockSpec(none + (bk, bn), lambda i, j, k: lead + (k, j)),
        ],
        out_specs=pl.BlockSpec((bm, bn), lambda i, j, k: (i, j), pipeline_mode=pl.Buffered(1)),
        out_shape=jax.ShapeDtypeStruct((m, n), F32),
        compiler_params=_params("arbitrary", "arbitrary", "arbitrary"),
        name="matmul_acc_w32",
    )(x, w_stack)


def _mm_w32_kernel(x_ref, wt_ref, o_ref):
    o_ref[...] = lax.dot_general(x_ref[...], wt_ref[...].astype(BF16), NT_DIMS, preferred_element_type=F32)


def _row_block(m):
    return _largest_divisor(m, 2176, BF16_SUBLANE)


def _mm_w32(x, wt_stack, layer, n_cols, *, bn=512):
    m, kd = x.shape
    bm = _row_block(m)
    bn = _largest_divisor(n_cols, bn, LANE)
    return pl.pallas_call(
        _mm_w32_kernel,
        grid=(m // bm, n_cols // bn),
        in_specs=[
            pl.BlockSpec((bm, kd), lambda i, j: (i, 0), pipeline_mode=pl.Buffered(1)),
            pl.BlockSpec((None, bn, kd), lambda i, j: (layer, j, 0)),
        ],
        out_specs=pl.BlockSpec((bm, bn), lambda i, j: (i, j)),
        out_shape=jax.ShapeDtypeStruct((m, n_cols), F32),
        compiler_params=_params("arbitrary", "arbitrary"),
        name="matmul_w32",
    )(x, wt_stack)


def _glu_kernel(x_ref, wg_ref, wu_ref, o_ref, *, nj_valid):
    j = pl.program_id(1)

    @pl.when(j < nj_valid)
    def _():
        x = x_ref[...]
        g = jnp.dot(x, wg_ref[...].astype(BF16), preferred_element_type=F32)
        u = jnp.dot(x, wu_ref[...].astype(BF16), preferred_element_type=F32)
        o_ref[...] = (_silu(g) * u).astype(BF16)

    @pl.when(j >= nj_valid)
    def _():
        o_ref[...] = jnp.zeros_like(o_ref)


def _glu(x, w_in, layer, i, f, fp):
    m, kd = x.shape
    bm = _row_block(m)
    bn = _largest_divisor(f, 256, LANE)
    assert fp % bn == 0
    nj_valid = f // bn
    last = nj_valid - 1
    return pl.pallas_call(
        functools.partial(_glu_kernel, nj_valid=nj_valid),
        grid=(m // bm, fp // bn),
        in_specs=[
            pl.BlockSpec((bm, kd), lambda r, j: (r, 0), pipeline_mode=pl.Buffered(1)),
            pl.BlockSpec((None, None, kd, bn), lambda r, j: (layer, i, 0, jnp.minimum(j, last))),
            pl.BlockSpec((None, None, kd, bn), lambda r, j: (layer, i, 0, nj_valid + jnp.minimum(j, last))),
        ],
        out_specs=pl.BlockSpec((bm, bn), lambda r, j: (r, j)),
        out_shape=jax.ShapeDtypeStruct((m, fp), BF16),
        compiler_params=_params("arbitrary", "arbitrary"),
        name="ffn_in_glu",
    )(x, w_in, w_in)


def _gelu_ln(v, ln_g, ln_b):
    gv = jax.nn.gelu(v)
    mu = jnp.mean(gv, axis=-1, keepdims=True)
    cen = gv - mu
    var = jnp.mean(cen * cen, axis=-1, keepdims=True)
    return cen * lax.rsqrt(var + EPS) * ln_g + ln_b


def _gmlp_p_kernel(u_ref, v_ref, lng_ref, lnb_ref, ws_ref, bs_ref, o_ref, *, n_heads):
    c = GM_CHUNK
    vn = _gelu_ln(v_ref[...], lng_ref[...], lnb_ref[...]).astype(BF16)
    tril = lax.broadcasted_iota(jnp.int32, (c, c), 0) >= lax.broadcasted_iota(jnp.int32, (c, c), 1)
    for h in range(n_heads):
        cols = slice(h * LANE, (h + 1) * LANE)
        w = jnp.where(tril, ws_ref[h], 0.0).astype(BF16)
        f = jnp.dot(w, vn[:, cols], preferred_element_type=F32) + bs_ref[:, cols]
        o_ref[:, cols] = (jax.nn.gelu(u_ref[:, cols]) * f).astype(BF16)


def _gmlp_prompt(p, tp, d, ln_g, ln_b, ws, bs_exp):
    n_heads = d // LANE
    c = GM_CHUNK
    m = p.shape[0]
    return pl.pallas_call(
        functools.partial(_gmlp_p_kernel, n_heads=n_heads),
        grid=(tp // c,),
        in_specs=[
            pl.BlockSpec((c, d), lambda i: (i, 0)),
            pl.BlockSpec((c, d), lambda i: (i, 1)),
            pl.BlockSpec((1, d), lambda i: (0, 0)),
            pl.BlockSpec((1, d), lambda i: (0, 0)),
            pl.BlockSpec((n_heads, c, c), lambda i: (0, 0, 0)),
            pl.BlockSpec((c, d), lambda i: (0, 0)),
        ],
        out_specs=pl.BlockSpec((c, d), lambda i: (i, 0)),
        out_shape=jax.ShapeDtypeStruct((m, 2 * d), BF16),
        compiler_params=_params("arbitrary"),
        name="gmlp_prompt",
    )(p, p, ln_g, ln_b, ws, bs_exp)


def _gmlp_s_kernel(u_ref, v_ref, lng_ref, lnb_ref, wsx_ref, bsx_ref, ya_ref, vn_ref, vnb, facc, *, t_len):
    t = pl.program_id(0)
    vn = _gelu_ln(v_ref[...], lng_ref[...], lnb_ref[...])
    vn_ref[...] = vn
    vnb[t] = vn.astype(BF16)
    facc[...] = jnp.broadcast_to(bsx_ref[t], facc.shape)
    w_t = wsx_ref[t]
    for s in range(t_len):
        @pl.when(s <= t)
        def _():
            w = w_t[s:s + 1, :].astype(BF16).astype(F32)
            facc[...] += w * vnb[s].astype(F32)
    ya_ref[...] = (jax.nn.gelu(u_ref[...]) * facc[...]).astype(BF16)


def _gmlp_sample(uvz3, d, ln_g, ln_b, wsx, bsx):
    t_len, bs, _ = uvz3.shape
    return pl.pallas_call(
        functools.partial(_gmlp_s_kernel, t_len=t_len),
        grid=(t_len,),
        in_specs=[
            pl.BlockSpec((None, bs, d), lambda t: (t, 0, 0)),
            pl.BlockSpec((None, bs, d), lambda t: (t, 0, 1)),
            pl.BlockSpec((1, d), lambda t: (0, 0)),
            pl.BlockSpec((1, d), lambda t: (0, 0)),
            pl.BlockSpec((t_len, t_len, d), lambda t: (0, 0, 0)),
            pl.BlockSpec((t_len, 1, d), lambda t: (0, 0, 0)),
        ],
        out_specs=[
            pl.BlockSpec((None, bs, d), lambda t: (t, 0, 0)),
            pl.BlockSpec((None, bs, d), lambda t: (t, 0, 0)),
        ],
        out_shape=[
            jax.ShapeDtypeStruct((t_len, bs, d), BF16),
            jax.ShapeDtypeStruct((t_len, bs, d), F32),
        ],
        scratch_shapes=[pltpu.VMEM((t_len, bs, d), BF16), pltpu.VMEM((bs, d), F32)],
        compiler_params=_params("arbitrary"),
        name="gmlp_sample",
    )(uvz3, uvz3, ln_g, ln_b, wsx, bsx)


def _head_expand(gw, first_head=0):
    row = lax.broadcasted_iota(jnp.int32, (LANE, gw), 0)
    col = lax.broadcasted_iota(jnp.int32, (LANE, gw), 1)
    return (lax.shift_right_logical(col, 6) + first_head == row).astype(BF16)


def _split3(x):
    x1 = x.astype(BF16)
    r1 = x - x1.astype(F32)
    x2 = r1.astype(BF16)
    x3 = (r1 - x2.astype(F32)).astype(BF16)
    return x1, x2, x3


def _dot_f32_by_01(x, m01):
    r = x.shape[0]
    y = jnp.dot(jnp.concatenate(_split3(x), axis=0), m01, preferred_element_type=F32)
    return y[:r] + y[r:2 * r] + y[2 * r:]


def _dot_01_by_f32(m01, x):
    n = x.shape[1]
    y = jnp.dot(m01, jnp.concatenate(_split3(x), axis=1), preferred_element_type=F32)
    return y[:, :n] + y[:, n:2 * n] + y[:, 2 * n:]


def _gated_group_norm(y, z, ng):
    y = y * _silu(z)
    return (_rms(y) * ng).astype(BF16)


def _ssd_p_kernel(x_ref, b_ref, c_ref, z_ref, dt_ref, cwx_ref, cwb_ref, cwc_ref, cbx_ref, cbb_ref, cbc_ref,
                  dtb_ref, alog_ref, dskip_ref, ng_ref, ymix_ref,
                  y_ref, cx_out, cb_out, cc_out, h_out,
                  xpad, bpad, cpad, state_t, ybuf, *, hpg, n_chunks, gpb):
    del ymix_ref
    n = SSD_CHUNK
    tail = CONV_W - 1
    first = SUBLANE - tail
    gw = hpg * HEAD_P
    ci = pl.program_id(2)

    @pl.when(ci == 0)
    def _():
        state_t[...] = jnp.zeros_like(state_t)
        for pad in (xpad, bpad, cpad):
            pad[pl.ds(0, SUBLANE), :] = jnp.zeros((SUBLANE, pad.shape[1]), F32)

    def conv_act(src_ref, pad_ref, cw_ref, cb_ref, out_ref, cols):
        x = src_ref[:, cols]
        pad_ref[pl.ds(SUBLANE, n), cols] = x
        acc = cb_ref[:, cols] + cw_ref[0:1, cols] * pad_ref[pl.ds(first, n), cols]
        for k in range(1, tail):
            acc = acc + cw_ref[k:k + 1, cols] * pad_ref[pl.ds(first + k, n), cols]
        acc = acc + cw_ref[tail:tail + 1, cols] * x
        last = src_ref[pl.ds(n - tail, tail), cols]
        pad_ref[pl.ds(first, tail), cols] = last
        out_ref[:, cols] = last
        return _silu(acc)

    tril = lax.broadcasted_iota(jnp.int32, (n, n), 0) >= lax.broadcasted_iota(jnp.int32, (n, n), 1)
    dt = jax.nn.softplus(dt_ref[...] + dtb_ref[...])
    a_cum = _dot_01_by_f32(tril.astype(BF16), dt * (-jnp.exp(alog_ref[...])))
    a_cum_t = a_cum.T
    dt_acum = jnp.concatenate([dt, a_cum], axis=0)
    for s in range(gpb):
        xc = slice(s * gw, (s + 1) * gw)
        sc = slice(s * D_STATE, (s + 1) * D_STATE)
        xs = conv_act(x_ref, xpad, cwx_ref, cbx_ref, cx_out, xc)
        bm = conv_act(b_ref, bpad, cwb_ref, cbb_ref, cb_out, sc).astype(BF16)
        cm = conv_act(c_ref, cpad, cwc_ref, cbc_ref, cc_out, sc).astype(BF16)
        both_e = _dot_f32_by_01(dt_acum, _head_expand(gw, s * hpg))
        dt_e, acum_e = both_e[:n], both_e[n:]
        dtx = dt_e * xs
        w_end = (jnp.exp(acum_e[n - 1:n, :] - acum_e) * dtx).astype(BF16)
        dtx_b = dtx.astype(BF16)
        exp_acum = jnp.exp(acum_e)
        cb = lax.dot_general(cm, bm, NT_DIMS, preferred_element_type=F32)
        h_in = state_t[s]
        y_inter = jnp.dot(cm, h_in.astype(BF16), preferred_element_type=F32) * exp_acum
        new = lax.dot_general(bm, w_end, TN_DIMS, preferred_element_type=F32)
        state_t[s] = h_in * exp_acum[n - 1:n, :] + new
        for j in range(hpg):
            cols = slice(j * HEAD_P, (j + 1) * HEAD_P)
            head = s * hpg + j
            seg = a_cum[:, head:head + 1] - a_cum_t[head:head + 1, :]
            decay = jnp.exp(jnp.where(tril, seg, -jnp.inf))
            ybuf[:, s * gw + j * HEAD_P:s * gw + (j + 1) * HEAD_P] = jnp.dot(
                (cb * decay).astype(BF16), dtx_b[:, cols], preferred_element_type=F32)
        y = ybuf[:, xc] + y_inter + dskip_ref[:, xc] * xs
        y_ref[:, xc] = _gated_group_norm(y, z_ref[:, xc], ng_ref[:, xc])

    @pl.when(ci == n_chunks - 1)
    def _():
        for s in range(gpb):
            final = state_t[s].T
            for j in range(hpg):
                h_out[s * hpg + j] = final[j * HEAD_P:(j + 1) * HEAD_P, :]


def _ssd_prompt(p, dt_all, y_mix, bp, seq, d, cw8, cb1, dtb_a, alog_a, dskip_e, ng):
    g = N_GROUPS
    gpb = SSD_GROUPS_PER_BLOCK
    assert gpb == g
    gw = d // g
    hpg = gw // HEAD_P
    n = SSD_CHUNK
    nc = seq // n
    tail = CONV_W - 1
    xw, sw = gpb * gw, gpb * D_STATE
    b_off = d // sw
    c_off = b_off + g // gpb
    z_off = 2 * d // xw
    x_off = 3 * d // xw
    pb_off = 4 * d // sw
    pc_off = pb_off + g // gpb

    def rows(width, off):
        return pl.BlockSpec((n, width), lambda b, gi, c: (b * nc + c, off + gi))

    def par(r, width, off):
        return pl.BlockSpec((r, width), lambda b, gi, c: (0, off + gi))

    def heads():
        return pl.BlockSpec((1, LANE), lambda b, gi, c: (0, 0))

    def conv_out(width):
        return pl.BlockSpec((None, tail, width), lambda b, gi, c: (b, 0, gi))

    outs = pl.pallas_call(
        functools.partial(_ssd_p_kernel, hpg=hpg, n_chunks=nc, gpb=gpb),
        grid=(bp, g // gpb, nc),
        in_specs=[
            rows(xw, x_off), rows(sw, pb_off), rows(sw, pc_off), rows(xw, z_off),
            pl.BlockSpec((n, LANE), lambda b, gi, c: (b * nc + c, 0)),
            par(SUBLANE, xw, 0), par(SUBLANE, sw, b_off), par(SUBLANE, sw, c_off),
            par(1, xw, 0), par(1, sw, b_off), par(1, sw, c_off),
            heads(), heads(), par(1, xw, 0), par(1, xw, 0),
            pl.BlockSpec(memory_space=pl.ANY),
        ],
        out_specs=[
            pl.BlockSpec((n, xw), lambda b, gi, c: (b * nc + c, g // gpb + gi)),
            conv_out(xw), conv_out(sw), conv_out(sw),
            pl.BlockSpec((None, gpb * hpg, HEAD_P, D_STATE), lambda b, gi, c: (b, gi, 0, 0)),
        ],
        input_output_aliases={15: 0},
        out_shape=[
            jax.ShapeDtypeStruct(y_mix.shape, BF16),
            jax.ShapeDtypeStruct((bp, tail, d), F32),
            jax.ShapeDtypeStruct((bp, tail, g * D_STATE), F32),
            jax.ShapeDtypeStruct((bp, tail, g * D_STATE), F32),
            jax.ShapeDtypeStruct((bp, g * hpg, HEAD_P, D_STATE), F32),
        ],
        scratch_shapes=[
            pltpu.VMEM((SUBLANE + n, xw), F32),
            pltpu.VMEM((SUBLANE + n, sw), F32),
            pltpu.VMEM((SUBLANE + n, sw), F32),
            pltpu.VMEM((gpb, D_STATE, gw), F32),
            pltpu.VMEM((n, xw), F32),
        ],
        compiler_params=_params("arbitrary", "arbitrary", "arbitrary"),
        name="ssd_prompt",
    )(p, p, p, p, dt_all, cw8, cw8, cw8, cb1, cb1, cb1, dtb_a, alog_a, dskip_e, ng, y_mix)
    y_mix, cx, cb_, cc, h_t = outs
    return y_mix, jnp.concatenate([cx, cb_, cc], axis=-1), h_t


def _ssd_s_kernel(x_ref, b_ref, c_ref, z_ref, dt_ref, sx_ref, sb_ref, sc_ref,
                  cwx_ref, cwb_ref, cwc_ref, cbx_ref, cbb_ref, cbc_ref,
                  dtb_ref, alog_ref, dskip_ref, ng_ref, h0_ref, *rest, t_len, sb, chained):
    y_ref, cx_out, cb_out, cc_out, h_out, cbuf, bbuf, wbuf, yibuf = rest[1:] if chained else rest
    tail = CONV_W - 1

    def conv_act(src_ref, st_ref, cw_ref, cb_ref, out_ref):
        full = [st_ref[r] for r in range(tail)] + [src_ref[t] for t in range(t_len)]
        acts = []
        for t in range(t_len):
            acc = cb_ref[...] + cw_ref[0:1, :] * full[t]
            for k in range(1, CONV_W):
                acc = acc + cw_ref[k:k + 1, :] * full[t + k]
            acts.append(_silu(acc))
        for r in range(tail):
            out_ref[r] = full[t_len + r]
        return acts

    xs = conv_act(x_ref, sx_ref, cwx_ref, cbx_ref, cx_out)
    bm = conv_act(b_ref, sb_ref, cwb_ref, cbb_ref, cb_out)
    cm = conv_act(c_ref, sc_ref, cwc_ref, cbc_ref, cc_out)
    gw = xs[0].shape[1]

    a_neg = -jnp.exp(alog_ref[...])
    dts, acums = [], []
    run = None
    for t in range(t_len):
        dt = jax.nn.softplus(dt_ref[t] + dtb_ref[...])
        run = dt * a_neg if run is None else run + dt * a_neg
        dts.append(dt)
        acums.append(run)
    both_e = _dot_f32_by_01(jnp.concatenate(dts + acums, axis=0), _head_expand(gw))
    acum_e = [both_e[(t_len + t) * sb:(t_len + t + 1) * sb] for t in range(t_len)]
    dtx = [both_e[t * sb:(t + 1) * sb] * xs[t] for t in range(t_len)]
    a_last = acum_e[t_len - 1]

    n_panels = gw // LANE

    def put_rows(buf, row0, val):
        for p in range(n_panels):
            buf[p, pl.ds(row0, sb), :] = val[:, p * LANE:(p + 1) * LANE]

    def seq_rows(buf, q):
        return jnp.concatenate([buf[p, pl.ds(q, SUBLANE, stride=sb), :] for p in range(n_panels)], axis=1)

    zero_rows = (SUBLANE - t_len) * sb
    for buf in (cbuf, bbuf):
        buf[pl.ds(t_len * sb, zero_rows), :] = jnp.zeros((zero_rows, D_STATE), F32)
    wbuf[:, pl.ds(t_len * sb, zero_rows), :] = jnp.zeros((n_panels, zero_rows, LANE), F32)
    for t in range(t_len):
        cbuf[pl.ds(t * sb, sb), :] = cm[t]
        bbuf[pl.ds(t * sb, sb), :] = bm[t]
        put_rows(wbuf, t * sb, jnp.exp(a_last - acum_e[t]) * dtx[t])
    chunk_decay = jnp.exp(a_last)
    ones = jnp.ones((SUBLANE, D_STATE), BF16)
    for q in range(sb):
        c_q = cbuf[pl.ds(q, SUBLANE, stride=sb), :].astype(BF16)
        b_q = bbuf[pl.ds(q, SUBLANE, stride=sb), :].astype(BF16)
        w_q = seq_rows(wbuf, q).astype(BF16)
        h0 = h0_ref[q]
        y_q = lax.dot_general(c_q, h0.astype(BF16), NT_DIMS, preferred_element_type=F32)
        for p in range(n_panels):
            yibuf[p, pl.ds(q, SUBLANE, stride=sb), :] = y_q[:, p * LANE:(p + 1) * LANE]
        new = lax.dot_general(w_q, b_q, TN_DIMS, preferred_element_type=F32)
        cd = chunk_decay[q:q + 1, :]
        hi = cd.astype(BF16).astype(F32)
        cd8 = jnp.concatenate([hi, cd - hi, jnp.zeros((SUBLANE - 2, gw), F32)], axis=0).astype(BF16)
        cd_col = lax.dot_general(cd8, ones, TN_DIMS, preferred_element_type=F32)
        h_out[q] = h0 * cd_col + new

    for t in range(t_len):
        y = None
        for s in range(t + 1):
            cbts = jnp.sum(cm[t] * bm[s], axis=-1, keepdims=True)
            term = cbts * jnp.exp(acum_e[t] - acum_e[s]) * dtx[s]
            y = term if y is None else y + term
        y_inter = jnp.concatenate([yibuf[p, pl.ds(t * sb, sb), :] for p in range(n_panels)], axis=1)
        y = y + y_inter * jnp.exp(acum_e[t]) + dskip_ref[...] * xs[t]
        y_ref[t] = _gated_group_norm(y, z_ref[t], ng_ref[...])


def _ssd_sample(p3, dt3, conv3, state_ssm4, ssm_prev, layer, d, cw8, cb1, dtb_g, alog_g, dskip_e, ng):
    t_len, bs, _ = p3.shape
    depth = state_ssm4.shape[0]
    assert t_len <= SUBLANE
    g = N_GROUPS
    gw = d // g
    tail = CONV_W - 1
    sb = SUBLANE
    b_off = d // D_STATE
    c_off = b_off + g
    z_off = 2 * d // gw
    x_off = 3 * d // gw
    pb_off = 4 * d // D_STATE
    pc_off = pb_off + g
    chained = ssm_prev is not None

    def slab(r, width, off):
        return pl.BlockSpec((r, sb, width), lambda i, gi: (0, i, off + gi))

    def par(r, width, off):
        return pl.BlockSpec((r, width), lambda i, gi: (0, off + gi))

    def grp():
        return pl.BlockSpec((None, 1, LANE), lambda i, gi: (gi, 0, 0))

    state_spec = pl.BlockSpec((None, sb, gw, D_STATE), lambda i, gi: (layer, i, gi, 0))
    in_specs = [
        slab(t_len, gw, x_off), slab(t_len, D_STATE, pb_off), slab(t_len, D_STATE, pc_off),
        slab(t_len, gw, z_off), slab(t_len, LANE, 0),
        slab(tail, gw, 0), slab(tail, D_STATE, b_off), slab(tail, D_STATE, c_off),
        par(SUBLANE, gw, 0), par(SUBLANE, D_STATE, b_off), par(SUBLANE, D_STATE, c_off),
        par(1, gw, 0), par(1, D_STATE, b_off), par(1, D_STATE, c_off),
        grp(), grp(), par(1, gw, 0), par(1, gw, 0),
        state_spec,
    ]
    args = [p3, p3, p3, p3, dt3, conv3, conv3, conv3, cw8, cw8, cw8, cb1, cb1, cb1,
            dtb_g, alog_g, dskip_e, ng, state_ssm4]
    if chained:
        in_specs.append(pl.BlockSpec(memory_space=pl.ANY))
        args.append(ssm_prev)
    outs = pl.pallas_call(
        functools.partial(_ssd_s_kernel, t_len=t_len, sb=sb, chained=chained),
        grid=(bs // sb, g),
        in_specs=in_specs,
        out_specs=[
            slab(t_len, gw, 0), slab(tail, gw, 0), slab(tail, D_STATE, 0), slab(tail, D_STATE, 0),
            state_spec,
        ],
        input_output_aliases={len(args) - 1: 4} if chained else {},
        out_shape=[
            jax.ShapeDtypeStruct((t_len, bs, d), BF16),
            jax.ShapeDtypeStruct((tail, bs, d), F32),
            jax.ShapeDtypeStruct((tail, bs, g * D_STATE), F32),
            jax.ShapeDtypeStruct((tail, bs, g * D_STATE), F32),
            jax.ShapeDtypeStruct((depth, bs, d, D_STATE), F32),
        ],
        scratch_shapes=[
            pltpu.VMEM((SUBLANE * sb, D_STATE), F32),
            pltpu.VMEM((SUBLANE * sb, D_STATE), F32),
            pltpu.VMEM((gw // LANE, SUBLANE * sb, LANE), F32),
            pltpu.VMEM((gw // LANE, SUBLANE * sb, LANE), F32),
        ],
        compiler_params=_params("arbitrary", "arbitrary"),
        name="ssd_sample",
    )(*args)
    yb, cx, cb_, cc, h_t = outs
    return yb, jnp.concatenate([cx, cb_, cc], axis=-1), h_t


def _group_pad(v):
    hpg = v.shape[0] // N_GROUPS
    return jnp.pad(v.reshape(N_GROUPS, hpg), ((0, 0), (0, LANE - hpg))).reshape(N_GROUPS, 1, LANE)


def kernel(x_prompt, x_sample, state_conv, state_ssm, c_prompt, c_sample, w_ada, b_ada, norm_pre, norm_post,
           ffn_w_in, ffn_w_out, w_in_mix, w_out_mix, gm_ln_g, gm_ln_b, gm_ws, gm_bs, conv_w, conv_b,
           dt_bias, a_log, d_skip, ssm_norm_g):
    bp, seq, d = x_prompt.shape
    bs, t_len, _ = x_sample.shape
    depth = w_ada.shape[0]
    f = ffn_w_out.shape[2]
    h_b = dt_bias.shape[1]
    conv_dim = conv_w.shape[2]
    g = N_GROUPS
    hpg = h_b // g
    tail = CONV_W - 1
    rows = _Rows(bp, seq, bs, t_len, d)
    tp = rows.tp
    fp = -(-f // 1024) * 1024
    n_proj = 3 * d + conv_dim

    x = (x_prompt.reshape(tp, d), x_sample.transpose(1, 0, 2).reshape(rows.ts, d))
    mc = -(-(bs + bp) // BF16_SUBLANE) * BF16_SUBLANE
    c_all = jnp.concatenate([c_sample, c_prompt, jnp.zeros((mc - bs - bp, d), F32)], axis=0)
    mod = _ada(c_all, w_ada, b_ada)
    mod_p = mod[:, bs:bs + bp].reshape(depth * bp * 3 * N_SUB, 1, d)
    npre = norm_pre.reshape(depth * N_SUB, 1, d)
    npost = norm_post.reshape(depth * N_SUB, 1, d)
    state_ssm4 = state_ssm.reshape(depth, bs, h_b * HEAD_P, D_STATE)

    def ffn(h, layer, i):
        a = _glu(h, ffn_w_in, layer, i, f, fp)
        return _mm_acc_w32(a, ffn_w_out, (layer, i))

    w_mix_t = jnp.swapaxes(w_in_mix, 1, 2)
    outs = {k: [] for k in ("p_conv", "p_ssm", "s_conv", "s_v")}
    ssm_s = None
    h = _pre(rows, x, npre, mod_p, mod, 0, 0)
    for layer in range(depth):
        x, h = _post(rows, x, ffn(h, layer, 0), npre, npost, mod_p, mod, layer, 0, FFN_RES, (layer, 1))

        p = _mm_w32(h, w_mix_t, layer, n_proj)
        w_dt_t = w_mix_t[layer, n_proj:]
        dt_all = _mm_w32(h, jnp.pad(w_dt_t, ((0, LANE - h_b), (0, 0))).reshape(1, LANE, d), 0, LANE)
        w_dtp_t = jnp.pad(w_dt_t.reshape(g, hpg, d), ((0, 0), (0, LANE - hpg), (0, 0))).reshape(1, g * LANE, d)
        dt3 = _mm_w32(h[tp:], w_dtp_t, 0, g * LANE).reshape(t_len, bs, g * LANE)
        p3 = p[tp:].reshape(t_len, bs, n_proj)

        ln_g = gm_ln_g[layer].reshape(1, d)
        ln_b = gm_ln_b[layer].reshape(1, d)
        bs_exp = jnp.repeat(gm_bs[layer].T, LANE, axis=1)
        y_mix = _gmlp_prompt(p, tp, d, ln_g, ln_b, gm_ws[layer], bs_exp)
        wsx = jnp.repeat(gm_ws[layer][:, :t_len, :t_len].transpose(1, 2, 0), LANE, axis=2)
        bsx = jnp.repeat(gm_bs[layer][:, :t_len].T, LANE, axis=1).reshape(t_len, 1, d)
        ya_s, vn_s = _gmlp_sample(p3, d, ln_g, ln_b, wsx, bsx)

        cw8 = jnp.pad(conv_w[layer], ((0, SUBLANE - CONV_W), (0, 0)))
        cb1 = conv_b[layer].reshape(1, conv_dim)
        dtb_g, alog_g = _group_pad(dt_bias[layer]), _group_pad(a_log[layer])
        dskip_e = jnp.repeat(d_skip[layer], HEAD_P).reshape(1, d)
        ng = ssm_norm_g[layer].reshape(1, d)
        dtb_a = jnp.pad(dt_bias[layer], (0, LANE - h_b)).reshape(1, LANE)
        alog_a = jnp.pad(a_log[layer], (0, LANE - h_b)).reshape(1, LANE)
        y_mix, conv_p, ssm_p = _ssd_prompt(p, dt_all, y_mix, bp, seq, d, cw8, cb1, dtb_a, alog_a, dskip_e, ng)
        conv3 = state_conv[layer].transpose(1, 0, 2)
        yb_s, conv_s, ssm_s = _ssd_sample(p3, dt3, conv3, state_ssm4, ssm_s, layer, d,
                                          cw8, cb1, dtb_g, alog_g, dskip_e, ng)
        y_s = jnp.concatenate([ya_s.reshape(rows.ts, d), yb_s.reshape(rows.ts, d)], axis=1)
        y_mix = lax.dynamic_update_slice(y_mix, y_s, (tp, 0))

        o = _mm_acc_w32(y_mix, w_out_mix, (layer,))
        x, h = _post(rows, x, o, npre, npost, mod_p, mod, layer, 1, 1.0, (layer, 2))

        nxt = (layer + 1, 0) if layer + 1 < depth else None
        x, h = _post(rows, x, ffn(h, layer, 1), npre, npost, mod_p, mod, layer, 2, FFN_RES, nxt)

        outs["p_conv"].append(conv_p)
        outs["p_ssm"].append(ssm_p)
        outs["s_conv"].append(conv_s.transpose(1, 0, 2))
        outs["s_v"].append(vn_s.transpose(1, 0, 2))

    y_prompt = x[0].reshape(bp, seq, d)
    y_sample = x[1].reshape(t_len, bs, d).transpose(1, 0, 2)
    sample_ssm = ssm_s.reshape(depth, bs, h_b, HEAD_P, D_STATE)
    return (y_prompt, y_sample, jnp.stack(outs["p_conv"]), jnp.stack(outs["p_ssm"]),
            jnp.stack(outs["s_conv"]), sample_ssm, jnp.stack(outs["s_v"]))
```
